```python
import math
import jax, jax.numpy as jnp
from jax import lax
import numpy as np

D_MODEL = 1024
BATCH = 1
SEQ = 16384
DEPTH = 4

HEAD_DIM = 64
N_HEADS_SB = 4
N_HEADS_DIL = 12
W_SB = N_HEADS_SB * HEAD_DIM
W_DIL = N_HEADS_DIL * HEAD_DIM
MIX_WIDTH = W_SB + W_DIL
IN_COLS = 4 * W_SB + 4 * W_DIL
DIL_PATTERNS = ((128, 1), (512, 4), (2048, 16))
ROPE_THETA = 500000.0
ROPE_DIM = HEAD_DIM // 4
BLOCK = 128
EPS = 1e-6

kernel_name = "hybrid_stickbreak_dilated_gated"


def _rmsnorm(x, g):
    xf = x.astype(jnp.float32)
    y = xf * lax.rsqrt(jnp.mean(xf * xf, axis=-1, keepdims=True) + EPS)
    return (y * g.astype(jnp.float32)).astype(x.dtype)


def _partial_rope(x, pos):
    half = ROPE_DIM // 2
    inv_freq = 1.0 / (ROPE_THETA ** (jnp.arange(half, dtype=jnp.float32) * 2.0 / ROPE_DIM))
    ang = pos.astype(jnp.float32)[:, None] * inv_freq[None, :]
    cos = jnp.cos(ang)[None, :, None, :]
    sin = jnp.sin(ang)[None, :, None, :]
    xf = x.astype(jnp.float32)
    x1 = xf[..., :half]
    x2 = xf[..., half:ROPE_DIM]
    rot = jnp.concatenate([x1 * cos - x2 * sin, x2 * cos + x1 * sin], axis=-1)
    return jnp.concatenate([rot, xf[..., ROPE_DIM:]], axis=-1).astype(x.dtype)


def _stick_breaking(q, k, v):
    B, S, H, Dh = q.shape
    nblk = S // BLOCK
    scale = 1.0 / math.sqrt(Dh)
    qt = q.transpose(0, 2, 1, 3)
    kt = k.transpose(0, 2, 1, 3)
    vt = v.transpose(0, 2, 1, 3)
    idx = jnp.arange(BLOCK)
    tri_in = (idx[:, None] > idx[None, :]).astype(jnp.float32)
    blk_idx = jnp.arange(nblk)
    tri_blk = (blk_idx[:, None] > blk_idx[None, :]).astype(jnp.float32)
    outs = []
    for i in range(nblk):
        nk = i + 1
        lk = nk * BLOCK
        qblk = qt[:, :, i * BLOCK:(i + 1) * BLOCK]
        z = jnp.einsum('bhqd,bhkd->bhqk', qblk, kt[:, :, :lk],
                       preferred_element_type=jnp.float32) * scale
        qpos = i * BLOCK + idx
        mask = jnp.arange(lk)[None, :] < qpos[:, None]
        ls_z = jax.nn.log_sigmoid(z)
        log_surv = jnp.where(mask, ls_z - z, 0.0)
        lsb = log_surv.reshape(B, H, BLOCK, nk, BLOCK)
        within = jnp.einsum('bhqnj,js->bhqns', lsb, tri_in,
                            precision=lax.Precision.HIGHEST)
        totals = jnp.sum(lsb, axis=-1)
        later = jnp.einsum('bhqm,mn->bhqn', totals, tri_blk[:nk, :nk],
                           precision=lax.Precision.HIGHEST)
        acc = (within + later[..., None]).reshape(B, H, BLOCK, lk)
        a = jnp.where(mask, jnp.exp(ls_z + acc), 0.0)
        o = jnp.einsum('bhqk,bhkd->bhqd', a, vt[:, :, :lk],
                       preferred_element_type=jnp.float32)
        outs.append(o.astype(q.dtype))
    o = jnp.concatenate(outs, axis=2)
    return o.transpose(0, 2, 1, 3)


def _dilated_window(q, k, v, window, dilation):
    B, S, H, Dh = q.shape
    r = dilation
    span = window // r
    L = S // r
    nb = -(-L // BLOCK)
    Lp = nb * BLOCK
    scale = 1.0 / math.sqrt(Dh)

    def to_blocks(t):
        t = t.reshape(B, L, r, H, Dh).transpose(0, 2, 1, 3, 4)
        t = jnp.pad(t, ((0, 0), (0, 0), (0, Lp - L), (0, 0), (0, 0)))
        return t.reshape(B, r, nb, BLOCK, H, Dh)

    def with_prev(t):
        prev = jnp.concatenate([jnp.zeros_like(t[:, :, :1]), t[:, :, :-1]], axis=2)
        return jnp.concatenate([prev, t], axis=3)

    qb = to_blocks(q)
    kk = with_prev(to_blocks(k))
    vv = with_prev(to_blocks(v))

    i = jnp.arange(BLOCK)[:, None]
    j = jnp.arange(2 * BLOCK)[None, :]
    n = jnp.arange(nb)[:, None, None]
    dist = BLOCK + i - j
    mask = (dist >= 0) & (dist <= span) & ((n > 0) | (j >= BLOCK))
    mask = mask[None, None, :, None]

    s = jnp.einsum('brnqhd,brnkhd->brnhqk', qb, kk,
                   preferred_element_type=jnp.float32) * scale
    s = jnp.where(mask, s, -jnp.inf)
    m = jnp.max(s, axis=-1, keepdims=True)
    p = jnp.exp(s - m)
    l = jnp.sum(p, axis=-1, keepdims=True)
    o = jnp.einsum('brnhqk,brnkhd->brnqhd', p, vv,
                   preferred_element_type=jnp.float32)
    o = o / jnp.transpose(l, (0, 1, 2, 4, 3, 5))
    log_den = (m + jnp.log(l))[..., 0].transpose(0, 1, 2, 4, 3)

    o = o.reshape(B, r, Lp, H, Dh)[:, :, :L].transpose(0, 2, 1, 3, 4).reshape(B, S, H, Dh)
    log_den = log_den.reshape(B, r, Lp, H)[:, :, :L].transpose(0, 2, 1, 3).reshape(B, S, H)
    return o, log_den


def _dilated_mixture(q, k, v):
    outs, dens = [], []
    for window, dilation in DIL_PATTERNS:
        o, ld = _dilated_window(q, k, v, window, dilation)
        outs.append(o)
        dens.append(ld)
    outs = jnp.stack(outs, axis=0)
    alpha = jax.nn.softmax(jnp.stack(dens, axis=0), axis=0)
    return jnp.sum(alpha[..., None] * outs, axis=0).astype(q.dtype)


def setup_inputs(seed: int = 0) -> dict:
    key = jax.random.key(seed)
    ks = jax.random.split(key, 6)
    x = jax.random.normal(ks[0], (BATCH, SEQ, D_MODEL), jnp.float32)
    norm_g = 1.0 + 0.02 * jax.random.normal(ks[1], (DEPTH, D_MODEL), jnp.float32)
    w_in = jax.random.normal(ks[2], (DEPTH, D_MODEL, IN_COLS), jnp.float32) * D_MODEL ** -0.5
    q_norm_g = 1.0 + 0.02 * jax.random.normal(ks[3], (DEPTH, HEAD_DIM), jnp.float32)
    k_norm_g = 1.0 + 0.02 * jax.random.normal(ks[4], (DEPTH, HEAD_DIM), jnp.float32)
    w_out = jax.random.normal(ks[5], (DEPTH, MIX_WIDTH, D_MODEL), jnp.float32) * (
        MIX_WIDTH ** -0.5 * (2 * DEPTH) ** -0.5)
    return {"x": x, "norm_g": norm_g, "w_in": w_in, "q_norm_g": q_norm_g,
            "k_norm_g": k_norm_g, "w_out": w_out}


def reference(x, norm_g, w_in, q_norm_g, k_norm_g, w_out):
    B, S, _ = x.shape
    pos = jnp.arange(S)
    cuts = np.cumsum([W_SB] * 4 + [W_DIL] * 3).tolist()
    for layer in range(DEPTH):
        h = _rmsnorm(x, norm_g[layer])
        proj = jnp.einsum('bsd,dc->bsc', h, w_in[layer])
        qa, ka, va, ga, qd, kd, vd, gd = jnp.split(proj, cuts, axis=-1)
        heads = lambda t, n: t.reshape(B, S, n, HEAD_DIM)

        oa = _stick_breaking(heads(qa, N_HEADS_SB), heads(ka, N_HEADS_SB), heads(va, N_HEADS_SB))
        oa = oa.reshape(B, S, W_SB) * jax.nn.silu(ga)

        qd = _partial_rope(_rmsnorm(heads(qd, N_HEADS_DIL), q_norm_g[layer]), pos)
        kd = _partial_rope(_rmsnorm(heads(kd, N_HEADS_DIL), k_norm_g[layer]), pos)
        od = _dilated_mixture(qd, kd, heads(vd, N_HEADS_DIL))
        od = od.reshape(B, S, W_DIL) * jax.nn.silu(gd)

        y = jnp.einsum('bsc,cd->bsd', jnp.concatenate([oa, od], axis=-1), w_out[layer])
        x = x + y.astype(x.dtype)
    return x
```

```python
import functools
import math

import jax
import jax.numpy as jnp
from jax import lax
from jax.experimental import pallas as pl
from jax.experimental.pallas import tpu as pltpu

HEAD_DIM = 64
N_HEADS_SB = 4
N_HEADS_DIL = 12
W_SB = N_HEADS_SB * HEAD_DIM
W_DIL = N_HEADS_DIL * HEAD_DIM
DIL_PATTERNS = ((128, 1), (512, 4), (2048, 16))
ROPE_THETA = 500000.0
ROPE_DIM = HEAD_DIM // 4
ROPE_HALF = ROPE_DIM // 2
BLOCK = 128
EPS = 1e-6
QK_SCALE = 1.0 / math.sqrt(HEAD_DIM)

LANES = 128
UNIT = 16 * BLOCK
N_PAIR_DIL = W_DIL // LANES
N_PAIR_SB = W_SB // LANES
ROW_TILE = 512
VMEM_LIMIT_BYTES = 56 * 1024 * 1024

LOG_SURVIVAL_FLOOR = -90.0
NEG_BIG = -1e30


def _split_bf16(x):
    hi = x.astype(jnp.bfloat16)
    lo = (x - hi.astype(jnp.float32)).astype(jnp.bfloat16)
    return hi, lo


def _lane_iota(shape):
    return lax.broadcasted_iota(jnp.int32, shape, 1)


def _row_iota(shape):
    return lax.broadcasted_iota(jnp.int32, shape, 0)


def _silu(x):
    return x * (1.0 / (1.0 + jnp.exp(-x)))


def _in_proj_kernel(x_ref, g_ref, w_ref, qg_ref, kg_ref, cos_ref, sa_ref, sb_ref,
                    qa_ref, ka_ref, va_ref, gate_ref,
                    q1_ref, k1_ref, v1_ref, q4_ref, k4_ref, v4_ref, q16_ref, k16_ref, v16_ref,
                    slab_ref):
    tm = x_ref.shape[0]
    x = x_ref[...]
    ms = jnp.mean(x * x, axis=-1, keepdims=True)
    h = (x * lax.rsqrt(ms + EPS) * g_ref[...]).astype(jnp.bfloat16)

    def proj(c0, width):
        return jnp.dot(h, w_ref[:, c0:c0 + width], preferred_element_type=jnp.float32)

    qa_ref[...] = (proj(0, W_SB) * QK_SCALE).astype(jnp.bfloat16)
    ka_ref[...] = proj(W_SB, W_SB).astype(jnp.bfloat16)
    va_ref[...] = proj(2 * W_SB, W_SB).astype(jnp.bfloat16)
    gate_ref[:, 0:W_SB] = _silu(proj(3 * W_SB, W_SB)).astype(jnp.bfloat16)
    base = 4 * W_SB
    gate_ref[:, W_SB:] = _silu(proj(base + 3 * W_DIL, W_DIL)).astype(jnp.bfloat16)

    r = _row_iota((LANES, LANES)) // HEAD_DIM
    c = _lane_iota((LANES, LANES)) // HEAD_DIM
    head_mean = jnp.where(r == c, 1.0 / HEAD_DIM, 0.0).astype(jnp.bfloat16)

    cos = cos_ref[...]
    sin_a = sa_ref[...]
    sin_b = sb_ref[...]

    def emit(pair, t, o1_ref, o4_ref, o16_ref):
        o1_ref[pair] = t.astype(jnp.bfloat16)
        slab_ref[...] = t
        n4 = tm // 4
        for cls in range(4):
            o4_ref[pair, cls] = slab_ref[pl.ds(cls, n4, stride=4), :].astype(jnp.bfloat16)
        n16 = tm // 16
        for cls in range(16):
            o16_ref[pair, cls] = slab_ref[pl.ds(cls, n16, stride=16), :].astype(jnp.bfloat16)

    def qk_norm_rope(t, gain, scale):
        hi, lo = _split_bf16(t * t)
        msq = (jnp.dot(hi, head_mean, preferred_element_type=jnp.float32)
               + jnp.dot(lo, head_mean, preferred_element_type=jnp.float32))
        t = t * lax.rsqrt(msq + EPS) * gain
        up = pltpu.roll(t, LANES - ROPE_HALF, axis=1)
        down = pltpu.roll(t, ROPE_HALF, axis=1)
        t = t * cos + up * sin_a + down * sin_b
        if scale != 1.0:
            t = t * scale
        return t

    for pair in range(N_PAIR_DIL):
        c0 = base + pair * LANES
        t = qk_norm_rope(proj(c0, LANES), qg_ref[...], QK_SCALE)
        emit(pair, t, q1_ref, q4_ref, q16_ref)
        t = qk_norm_rope(proj(c0 + W_DIL, LANES), kg_ref[...], 1.0)
        emit(pair, t, k1_ref, k4_ref, k16_ref)
        emit(pair, proj(c0 + 2 * W_DIL, LANES), v1_ref, v4_ref, v16_ref)


def _in_proj(x, g, w_bf16, qg, kg, cos_t, sin_a, sin_b):
    s, d = x.shape
    tm = ROW_TILE
    per_unit = UNIT // tm
    n_units = s // UNIT
    bf = jnp.bfloat16
    row = lambda i: (i, 0)
    const = lambda i: (0, 0)
    nat_spec = pl.BlockSpec((N_PAIR_DIL, tm, LANES), lambda i: (0, i, 0))
    c4_spec = pl.BlockSpec((N_PAIR_DIL, None, 4, tm // 4, LANES),
                           lambda i: (0, i // per_unit, 0, i % per_unit, 0))
    c16_spec = pl.BlockSpec((N_PAIR_DIL, None, 16, tm // 16, LANES),
                            lambda i: (0, i // per_unit, 0, i % per_unit, 0))
    nat_shape = jax.ShapeDtypeStruct((N_PAIR_DIL, s, LANES), bf)
    c4_shape = jax.ShapeDtypeStruct((N_PAIR_DIL, n_units, 4, UNIT // 4, LANES), bf)
    c16_shape = jax.ShapeDtypeStruct((N_PAIR_DIL, n_units, 16, UNIT // 16, LANES), bf)
    sb_shape = jax.ShapeDtypeStruct((s, W_SB), bf)
    return pl.pallas_call(
        _in_proj_kernel,
        grid=(s // tm,),
        in_specs=[
            pl.BlockSpec((tm, d), row),
            pl.BlockSpec((1, d), const),
            pl.BlockSpec(w_bf16.shape, const),
            pl.BlockSpec((1, LANES), const),
            pl.BlockSpec((1, LANES), const),
            pl.BlockSpec((tm, LANES), row),
            pl.BlockSpec((tm, LANES), row),
            pl.BlockSpec((tm, LANES), row),
        ],
        out_specs=[
            pl.BlockSpec((tm, W_SB), row), pl.BlockSpec((tm, W_SB), row),
            pl.BlockSpec((tm, W_SB), row), pl.BlockSpec((tm, W_SB + W_DIL), row),
            nat_spec, nat_spec, nat_spec, c4_spec, c4_spec, c4_spec,
            c16_spec, c16_spec, c16_spec,
        ],
        out_shape=[sb_shape, sb_shape, sb_shape,
                   jax.ShapeDtypeStruct((s, W_SB + W_DIL), bf),
                   nat_shape, nat_shape, nat_shape, c4_shape, c4_shape, c4_shape,
                   c16_shape, c16_shape, c16_shape],
        scratch_shapes=[pltpu.VMEM((tm, LANES), jnp.float32)],
        compiler_params=pltpu.CompilerParams(
            dimension_semantics=("arbitrary",), vmem_limit_bytes=VMEM_LIMIT_BYTES),
        name="in_proj",
    )(x, g, w_bf16, qg, kg, cos_t, sin_a, sin_b)


def _stack_heads(q):
    lane = _lane_iota(q.shape)
    zero = jnp.zeros_like(q)
    return jnp.concatenate([jnp.where(lane < HEAD_DIM, q, zero),
                            jnp.where(lane >= HEAD_DIM, q, zero)], axis=0)


def _unstack_heads(t):
    top, bot = t[:BLOCK], t[BLOCK:]
    return jnp.where(_lane_iota(top.shape) < HEAD_DIM, top, bot)


def _sb_attn_kernel(q_ref, k_ref, v_ref, gate_ref, o_ref, acc_ref, later_ref):
    i = pl.program_id(1)
    qs = _stack_heads(q_ref[...])
    rows = _row_iota((2 * BLOCK, BLOCK)) % BLOCK
    cols = _lane_iota((2 * BLOCK, BLOCK))
    strictly_before = cols < rows
    tri = (_row_iota((BLOCK, BLOCK)) > _lane_iota((BLOCK, BLOCK))).astype(jnp.bfloat16)

    acc_ref[...] = jnp.zeros_like(acc_ref)
    later_ref[...] = jnp.zeros_like(later_ref)

    def key_block(j, diagonal):
        start = pl.multiple_of(j * BLOCK, BLOCK)
        kb = k_ref[pl.ds(start, BLOCK), :]
        vb = v_ref[pl.ds(start, BLOCK), :]
        z = lax.dot_general(qs, kb, (((1,), (1,)), ((), ())),
                            preferred_element_type=jnp.float32)
        soft = jnp.log(1.0 + jnp.exp(-jnp.abs(z)))
        log_surv = -(jnp.maximum(z, 0.0) + soft)
        log_beta = jnp.minimum(z, 0.0) - soft
        if diagonal:
            log_surv = jnp.where(strictly_before, log_surv, 0.0)
        hi, lo = _split_bf16(log_surv)
        within = (jnp.dot(hi, tri, preferred_element_type=jnp.float32)
                  + jnp.dot(lo, tri, preferred_element_type=jnp.float32))
        later = later_ref[...]
        a = jnp.exp(log_beta + within + later)
        if diagonal:
            a = jnp.where(strictly_before, a, 0.0)
        acc_ref[...] += jnp.dot(a.astype(jnp.bfloat16), vb, preferred_element_type=jnp.float32)
        later_ref[...] = later + jnp.sum(log_surv, axis=-1, keepdims=True)

    key_block(i, True)

    def cond(carry):
        j, live = carry
        return jnp.logical_and(j >= 0, live)

    def body(carry):
        j, _ = carry
        key_block(j, False)
        return j - 1, jnp.max(later_ref[...]) > LOG_SURVIVAL_FLOOR

    lax.while_loop(cond, body, (i - 1, True))

    o = _unstack_heads(acc_ref[...])
    o_ref[...] = (o * gate_ref[...].astype(jnp.float32)).astype(o_ref.dtype)


def _sb_attn(qa, ka, va, gate):
    s = qa.shape[0]
    blk = lambda p, i: (i, p)
    full = lambda p, i: (0, p)
    return pl.pallas_call(
        _sb_attn_kernel,
        grid=(N_PAIR_SB, s // BLOCK),
        in_specs=[
            pl.BlockSpec((BLOCK, LANES), blk),
            pl.BlockSpec((s, LANES), full),
            pl.BlockSpec((s, LANES), full),
            pl.BlockSpec((BLOCK, LANES), blk),
        ],
        out_specs=pl.BlockSpec((BLOCK, LANES), blk),
        out_shape=jax.ShapeDtypeStruct((s, W_SB), jnp.bfloat16),
        scratch_shapes=[pltpu.VMEM((2 * BLOCK, LANES), jnp.float32),
                        pltpu.VMEM((2 * BLOCK, LANES), jnp.float32)],
        compiler_params=pltpu.CompilerParams(
            dimension_semantics=("arbitrary", "arbitrary"), vmem_limit_bytes=VMEM_LIMIT_BYTES),
        name="sb_attn",
    )(qa, ka, va, gate)


def _band_attend(q, k_prev, k_own, v_prev, v_own, has_prev):
    qs = _stack_heads(q)
    kk = jnp.concatenate([k_prev, k_own], axis=0)
    vv = jnp.concatenate([v_prev, v_own], axis=0)
    s = lax.dot_general(qs, kk, (((1,), (1,)), ((), ())),
                        preferred_element_type=jnp.float32)
    i = _row_iota(s.shape) % BLOCK
    j = _lane_iota(s.shape)
    band = jnp.logical_and(j >= i, j <= i + BLOCK)
    band = jnp.logical_and(band, jnp.logical_or(j >= BLOCK, has_prev))
    s = jnp.where(band, s, NEG_BIG)
    m = jnp.max(s, axis=-1, keepdims=True)
    p = jnp.exp(s - m)
    l = jnp.sum(p, axis=-1, keepdims=True)
    o = jnp.dot(p.astype(jnp.bfloat16), vv, preferred_element_type=jnp.float32) / l
    log_den = jnp.broadcast_to(m + jnp.log(l), o.shape)
    return _unstack_heads(o), _unstack_heads(log_den)


def _dil_attn_kernel(q1_ref, k1_ref, v1_ref, k1p_ref, v1p_ref,
                     q4_ref, k4_ref, v4_ref, k4p_ref, v4p_ref,
                     q16_ref, k16_ref, v16_ref, k16p_ref, v16p_ref,
                     gate_ref, o_ref, out_s, den_s):
    not_first_unit = pl.program_id(0) > 0
    blocks_per_unit = UNIT // BLOCK

    def store1(n, o, d):
        start = pl.multiple_of(n * BLOCK, BLOCK)
        out_s[0, pl.ds(start, BLOCK), :] = o
        den_s[0, pl.ds(start, BLOCK), :] = d

    o, d = _band_attend(q1_ref[0:BLOCK], k1p_ref[...], k1_ref[0:BLOCK],
                        v1p_ref[...], v1_ref[0:BLOCK], not_first_unit)
    store1(0, o, d)

    def body1(n, carry):
        cur = pl.ds(pl.multiple_of(n * BLOCK, BLOCK), BLOCK)
        prv = pl.ds(pl.multiple_of((n - 1) * BLOCK, BLOCK), BLOCK)
        o, d = _band_attend(q1_ref[cur, :], k1_ref[prv, :], k1_ref[cur, :],
                            v1_ref[prv, :], v1_ref[cur, :], True)
        store1(n, o, d)
        return carry

    lax.fori_loop(1, blocks_per_unit, body1, 0)

    per_class4 = blocks_per_unit // 4

    def store4(cls, n, o, d):
        rows = pl.ds(cls + 4 * BLOCK * n, BLOCK, stride=4)
        out_s[1, rows, :] = o
        den_s[1, rows, :] = d

    def body4(cls, carry):
        o, d = _band_attend(q4_ref[cls, 0:BLOCK], k4p_ref[cls], k4_ref[cls, 0:BLOCK],
                            v4p_ref[cls], v4_ref[cls, 0:BLOCK], not_first_unit)
        store4(cls, 0, o, d)
        for n in range(1, per_class4):
            cur = slice(n * BLOCK, (n + 1) * BLOCK)
            prv = slice((n - 1) * BLOCK, n * BLOCK)
            o, d = _band_attend(q4_ref[cls, cur], k4_ref[cls, prv], k4_ref[cls, cur],
                                v4_ref[cls, prv], v4_ref[cls, cur], True)
            store4(cls, n, o, d)
        return carry

    lax.fori_loop(0, 4, body4, 0)

    def body16(cls, carry):
        o, d = _band_attend(q16_ref[cls], k16p_ref[cls], k16_ref[cls],
                            v16p_ref[cls], v16_ref[cls], not_first_unit)
        rows = pl.ds(cls, BLOCK, stride=16)
        out_s[2, rows, :] = o
        den_s[2, rows, :] = d
        return carry

    lax.fori_loop(0, 16, body16, 0)

    chunk = 256
    for c0 in range(0, UNIT, chunk):
        rows = slice(c0, c0 + chunk)
        d0, d1, d2 = den_s[0, rows, :], den_s[1, rows, :], den_s[2, rows, :]
        mx = jnp.maximum(jnp.maximum(d0, d1), d2)
        w0, w1, w2 = jnp.exp(d0 - mx), jnp.exp(d1 - mx), jnp.exp(d2 - mx)
        mix = (w0 * out_s[0, rows, :] + w1 * out_s[1, rows, :] + w2 * out_s[2, rows, :]) / (w0 + w1 + w2)
        o_ref[rows, :] = (mix * gate_ref[rows, :].astype(jnp.float32)).astype(o_ref.dtype)


def _dil_attn(q1, k1, v1, q4, k4, v4, q16, k16, v16, gate):
    s = q1.shape[1]
    n_units = s // UNIT
    blocks_per_unit = UNIT // BLOCK
    n4 = UNIT // 4
    prev_unit = lambda u: jnp.maximum(u - 1, 0)
    nat = pl.BlockSpec((None, UNIT, LANES), lambda u, p: (p, u, 0))
    nat_prev = pl.BlockSpec((None, BLOCK, LANES),
                            lambda u, p: (p, jnp.maximum(u * blocks_per_unit - 1, 0), 0))
    c4 = pl.BlockSpec((None, None, 4, n4, LANES), lambda u, p: (p, u, 0, 0, 0))
    c4_prev = pl.BlockSpec((None, None, 4, BLOCK, LANES),
                           lambda u, p: (p, prev_unit(u), 0, n4 // BLOCK - 1, 0))
    c16 = pl.BlockSpec((None, None, 16, BLOCK, LANES), lambda u, p: (p, u, 0, 0, 0))
    c16_prev = pl.BlockSpec((None, None, 16, BLOCK, LANES), lambda u, p: (p, prev_unit(u), 0, 0, 0))
    return pl.pallas_call(
        _dil_attn_kernel,
        grid=(n_units, N_PAIR_DIL),
        in_specs=[nat, nat, nat, nat_prev, nat_prev,
                  c4, c4, c4, c4_prev, c4_prev,
                  c16, c16, c16, c16_prev, c16_prev,
                  pl.BlockSpec((UNIT, LANES), lambda u, p: (u, N_PAIR_SB + p))],
        out_specs=pl.BlockSpec((UNIT, LANES), lambda u, p: (u, p)),
        out_shape=jax.ShapeDtypeStruct((s, W_DIL), jnp.bfloat16),
        scratch_shapes=[pltpu.VMEM((3, UNIT, LANES), jnp.float32),
                        pltpu.VMEM((3, UNIT, LANES), jnp.float32)],
        compiler_params=pltpu.CompilerParams(
            dimension_semantics=("arbitrary", "arbitrary"), vmem_limit_bytes=VMEM_LIMIT_BYTES),
        name="dil_attn",
    )(q1, k1, v1, k1, v1, q4, k4, v4, k4, v4, q16, k16, v16, k16, v16, gate)


def _out_proj_kernel(x_ref, ma_ref, md_ref, w_ref, o_ref):
    y = jnp.dot(ma_ref[...], w_ref[0:W_SB, :], preferred_element_type=jnp.float32)
    y = y + jnp.dot(md_ref[...], w_ref[W_SB:, :], preferred_element_type=jnp.float32)
    o_ref[...] = x_ref[...] + y


def _out_proj(x, mix_a, mix_d, w_bf16):
    s, d = x.shape
    tm = ROW_TILE
    row = lambda i: (i, 0)
    return pl.pallas_call(
        _out_proj_kernel,
        grid=(s // tm,),
        in_specs=[pl.BlockSpec((tm, d), row), pl.BlockSpec((tm, W_SB), row),
                  pl.BlockSpec((tm, W_DIL), row), pl.BlockSpec(w_bf16.shape, lambda i: (0, 0))],
        out_specs=pl.BlockSpec((tm, d), row),
        out_shape=jax.ShapeDtypeStruct((s, d), x.dtype),
        compiler_params=pltpu.CompilerParams(
            dimension_semantics=("arbitrary",), vmem_limit_bytes=VMEM_LIMIT_BYTES),
        name="out_proj",
    )(x, mix_a, mix_d, w_bf16)


def _rope_tables(s):
    inv_freq = 1.0 / (ROPE_THETA ** (jnp.arange(ROPE_HALF, dtype=jnp.float32) * 2.0 / ROPE_DIM))
    ang = jnp.arange(s).astype(jnp.float32)[:, None] * inv_freq[None, :]
    cos, sin = jnp.cos(ang), jnp.sin(ang)
    rest = HEAD_DIM - ROPE_DIM
    zeros = lambda n: jnp.zeros((s, n), jnp.float32)
    cos_t = jnp.concatenate([cos, cos, jnp.ones((s, rest), jnp.float32)], axis=-1)
    sin_a = jnp.concatenate([-sin, zeros(HEAD_DIM - ROPE_HALF)], axis=-1)
    sin_b = jnp.concatenate([zeros(ROPE_HALF), sin, zeros(rest)], axis=-1)
    two = lambda t: jnp.tile(t, (1, LANES // HEAD_DIM))
    return two(cos_t), two(sin_a), two(sin_b)


def kernel(x, norm_g, w_in, q_norm_g, k_norm_g, w_out):
    b, s, d = x.shape
    assert s % UNIT == 0 and d == W_SB + W_DIL
    depth = w_in.shape[0]
    cos_t, sin_a, sin_b = _rope_tables(s)
    w_in_bf = w_in.astype(jnp.bfloat16)
    w_out_bf = w_out.astype(jnp.bfloat16)
    two = lambda g: jnp.tile(g.reshape(1, HEAD_DIM), (1, LANES // HEAD_DIM))
    outs = []
    for bi in range(b):
        xb = x[bi]
        for layer in range(depth):
            (qa, ka, va, gate, q1, k1, v1, q4, k4, v4, q16, k16, v16) = _in_proj(
                xb, norm_g[layer].reshape(1, d), w_in_bf[layer],
                two(q_norm_g[layer]), two(k_norm_g[layer]), cos_t, sin_a, sin_b)
            mix_a = _sb_attn(qa, ka, va, gate)
            mix_d = _dil_attn(q1, k1, v1, q4, k4, v4, q16, k16, v16, gate)
            xb = _out_proj(xb, mix_a, mix_d, w_out_bf[layer])
        outs.append(xb)
    return jnp.stack(outs, axis=0)
```

```python
import functools
import math

import jax
import jax.numpy as jnp
from jax import lax
from jax.experimental import pallas as pl
from jax.experimental.pallas import tpu as pltpu

HEAD_DIM = 64
N_HEADS_SB = 4
N_HEADS_DIL = 12
W_SB = N_HEADS_SB * HEAD_DIM
W_DIL = N_HEADS_DIL * HEAD_DIM
DIL_PATTERNS = ((128, 1), (512, 4), (2048, 16))
ROPE_THETA = 500000.0
ROPE_DIM = HEAD_DIM // 4
ROPE_HALF = ROPE_DIM // 2
BLOCK = 128
EPS = 1e-6
QK_SCALE = 1.0 / math.sqrt(HEAD_DIM)

LANES = 128
UNIT = 16 * BLOCK
N_PAIR_DIL = W_DIL // LANES
N_PAIR_SB = W_SB // LANES
ROW_TILE = 512
VMEM_LIMIT_BYTES = 56 * 1024 * 1024

LOG_SURVIVAL_FLOOR = -90.0
NEG_BIG = -1e30


def _split_bf16(x):
    hi = x.astype(jnp.bfloat16)
    lo = (x - hi.astype(jnp.float32)).astype(jnp.bfloat16)
    return hi, lo


def _lane_iota(shape):
    return lax.broadcasted_iota(jnp.int32, shape, 1)


def _row_iota(shape):
    return lax.broadcasted_iota(jnp.int32, shape, 0)


def _silu(x):
    return x * (1.0 / (1.0 + jnp.exp(-x)))


def _in_proj_kernel(x_ref, g_ref, w_ref, qg_ref, kg_ref, cos_ref, sa_ref, sb_ref,
                    qa_ref, ka_ref, va_ref, gate_ref,
                    q1_ref, k1_ref, v1_ref, q4_ref, k4_ref, v4_ref, q16_ref, k16_ref, v16_ref,
                    slab_ref):
    tm = x_ref.shape[0]
    x = x_ref[...]
    ms = jnp.mean(x * x, axis=-1, keepdims=True)
    h = (x * lax.rsqrt(ms + EPS) * g_ref[...]).astype(jnp.bfloat16)

    def proj(c0, width):
        return jnp.dot(h, w_ref[:, c0:c0 + width], preferred_element_type=jnp.float32)

    pa = proj(0, 4 * W_SB)
    qa_ref[...] = (pa[:, 0:W_SB] * QK_SCALE).astype(jnp.bfloat16)
    ka_ref[...] = pa[:, W_SB:2 * W_SB].astype(jnp.bfloat16)
    va_ref[...] = pa[:, 2 * W_SB:3 * W_SB].astype(jnp.bfloat16)
    gate_ref[:, 0:W_SB] = _silu(pa[:, 3 * W_SB:]).astype(jnp.bfloat16)
    base = 4 * W_SB
    gate_ref[:, W_SB:] = _silu(proj(base + 3 * W_DIL, W_DIL)).astype(jnp.bfloat16)

    r = (_row_iota((2 * LANES, LANES)) % LANES) // HEAD_DIM
    c = _lane_iota((2 * LANES, LANES)) // HEAD_DIM
    head_mean = jnp.where(r == c, 1.0 / HEAD_DIM, 0.0).astype(jnp.bfloat16)

    cos = cos_ref[...]
    sin_a = sa_ref[...]
    sin_b = sb_ref[...]

    def emit(slot, pair, t, o1_ref, o4_ref, o16_ref):
        o1_ref[pair] = t.astype(jnp.bfloat16)
        slab_ref[slot] = t
        n4 = tm // 4
        for cls in range(4):
            o4_ref[pair, cls] = slab_ref[slot, pl.ds(cls, n4, stride=4), :].astype(jnp.bfloat16)
        n16 = tm // 16
        for cls in range(16):
            o16_ref[pair, cls] = slab_ref[slot, pl.ds(cls, n16, stride=16), :].astype(jnp.bfloat16)

    def qk_norm_rope(t, gain, scale):
        hi, lo = _split_bf16(t * t)
        msq = jnp.dot(jnp.concatenate([hi, lo], axis=1), head_mean,
                      preferred_element_type=jnp.float32)
        t = t * lax.rsqrt(msq + EPS) * gain
        up = pltpu.roll(t, LANES - ROPE_HALF, axis=1)
        down = pltpu.roll(t, ROPE_HALF, axis=1)
        t = t * cos + up * sin_a + down * sin_b
        if scale != 1.0:
            t = t * scale
        return t

    pq = proj(base, W_DIL)
    for pair in range(N_PAIR_DIL):
        t = qk_norm_rope(pq[:, pair * LANES:(pair + 1) * LANES], qg_ref[...], QK_SCALE)
        emit(pair, pair, t, q1_ref, q4_ref, q16_ref)
    pk = proj(base + W_DIL, W_DIL)
    for pair in range(N_PAIR_DIL):
        t = qk_norm_rope(pk[:, pair * LANES:(pair + 1) * LANES], kg_ref[...], 1.0)
        emit(N_PAIR_DIL + pair, pair, t, k1_ref, k4_ref, k16_ref)
    pv = proj(base + 2 * W_DIL, W_DIL)
    for pair in range(N_PAIR_DIL):
        emit(2 * N_PAIR_DIL + pair, pair, pv[:, pair * LANES:(pair + 1) * LANES],
             v1_ref, v4_ref, v16_ref)


def _in_proj(x, g, w_bf16, qg, kg, cos_t, sin_a, sin_b):
    s, d = x.shape
    tm = ROW_TILE
    per_unit = UNIT // tm
    n_units = s // UNIT
    bf = jnp.bfloat16
    row = lambda i: (i, 0)
    const = lambda i: (0, 0)
    nat_spec = pl.BlockSpec((N_PAIR_DIL, tm, LANES), lambda i: (0, i, 0))
    c4_spec = pl.BlockSpec((N_PAIR_DIL, None, 4, tm // 4, LANES),
                           lambda i: (0, i // per_unit, 0, i % per_unit, 0))
    c16_spec = pl.BlockSpec((N_PAIR_DIL, None, 16, tm // 16, LANES),
                            lambda i: (0, i // per_unit, 0, i % per_unit, 0))
    nat_shape = jax.ShapeDtypeStruct((N_PAIR_DIL, s, LANES), bf)
    c4_shape = jax.ShapeDtypeStruct((N_PAIR_DIL, n_units, 4, UNIT // 4, LANES), bf)
    c16_shape = jax.ShapeDtypeStruct((N_PAIR_DIL, n_units, 16, UNIT // 16, LANES), bf)
    sb_shape = jax.ShapeDtypeStruct((s, W_SB), bf)
    return pl.pallas_call(
        _in_proj_kernel,
        grid=(s // tm,),
        in_specs=[
            pl.BlockSpec((tm, d), row),
            pl.BlockSpec((1, d), const),
            pl.BlockSpec(w_bf16.shape, const),
            pl.BlockSpec((1, LANES), const),
            pl.BlockSpec((1, LANES), const),
            pl.BlockSpec((tm, LANES), row),
            pl.BlockSpec((tm, LANES), row),
            pl.BlockSpec((tm, LANES), row),
        ],
        out_specs=[
            pl.BlockSpec((tm, W_SB), row), pl.BlockSpec((tm, W_SB), row),
            pl.BlockSpec((tm, W_SB), row), pl.BlockSpec((tm, W_SB + W_DIL), row),
            nat_spec, nat_spec, nat_spec, c4_spec, c4_spec, c4_spec,
            c16_spec, c16_spec, c16_spec,
        ],
        out_shape=[sb_shape, sb_shape, sb_shape,
                   jax.ShapeDtypeStruct((s, W_SB + W_DIL), bf),
                   nat_shape, nat_shape, nat_shape, c4_shape, c4_shape, c4_shape,
                   c16_shape, c16_shape, c16_shape],
        scratch_shapes=[pltpu.VMEM((3 * N_PAIR_DIL, tm, LANES), jnp.float32)],
        compiler_params=pltpu.CompilerParams(
            dimension_semantics=("arbitrary",), vmem_limit_bytes=VMEM_LIMIT_BYTES),
        name="in_proj",
    )(x, g, w_bf16, qg, kg, cos_t, sin_a, sin_b)


def _stack_heads(q):
    lane = _lane_iota(q.shape)
    zero = jnp.zeros_like(q)
    return jnp.concatenate([jnp.where(lane < HEAD_DIM, q, zero),
                            jnp.where(lane >= HEAD_DIM, q, zero)], axis=0)


def _unstack_heads(t):
    top, bot = t[:BLOCK], t[BLOCK:]
    return jnp.where(_lane_iota(top.shape) < HEAD_DIM, top, bot)


def _sb_tile(qs, kb, vb, later, mask):
    z = lax.dot_general(qs, kb, (((1,), (1,)), ((), ())), preferred_element_type=jnp.float32)
    soft = jnp.log(1.0 + jnp.exp(-jnp.abs(z)))
    log_surv = -(jnp.maximum(z, 0.0) + soft)
    log_beta = jnp.minimum(z, 0.0) - soft
    if mask is not None:
        log_surv = jnp.where(mask, log_surv, 0.0)
    tri = (_row_iota((BLOCK, BLOCK)) > _lane_iota((BLOCK, BLOCK))).astype(jnp.bfloat16)
    hi, lo = _split_bf16(log_surv)
    within = (jnp.dot(hi, tri, preferred_element_type=jnp.float32)
              + jnp.dot(lo, tri, preferred_element_type=jnp.float32))
    a = jnp.exp(log_beta + within + later)
    if mask is not None:
        a = jnp.where(mask, a, 0.0)
    pv = jnp.dot(a.astype(jnp.bfloat16), vb, preferred_element_type=jnp.float32)
    return pv, jnp.sum(log_surv, axis=-1, keepdims=True)


def _sb_attn_kernel(q_ref, k_ref, v_ref, gate_ref, o_ref, acc_ref, later_ref):
    i = pl.program_id(0)
    strictly_before = _lane_iota((2 * BLOCK, BLOCK)) < _row_iota((2 * BLOCK, BLOCK)) % BLOCK
    diag = pl.ds(pl.multiple_of(i * BLOCK, BLOCK), BLOCK)
    prev = pl.ds(pl.multiple_of(jnp.maximum(i - 1, 0) * BLOCK, BLOCK), BLOCK)
    no_prev = jnp.where(i > 0, 0.0, NEG_BIG)

    qs, live = [], []
    for pair in range(N_PAIR_SB):
        lanes = slice(pair * LANES, (pair + 1) * LANES)
        q = _stack_heads(q_ref[:, lanes])
        pv0, tot0 = _sb_tile(q, k_ref[diag, lanes], v_ref[diag, lanes], 0.0, strictly_before)
        pv1, tot1 = _sb_tile(q, k_ref[prev, lanes], v_ref[prev, lanes], tot0 + no_prev, None)
        later = tot0 + tot1
        acc_ref[pair] = pv0 + pv1
        later_ref[pair] = jnp.broadcast_to(later, later_ref.shape[1:])
        qs.append(q)
        live.append(later)

    def cond(carry):
        j, more = carry
        return jnp.logical_and(j >= 0, more)

    def body(carry):
        j, _ = carry
        rows = pl.ds(pl.multiple_of(j * BLOCK, BLOCK), BLOCK)
        worst = None
        for pair in range(N_PAIR_SB):
            lanes = slice(pair * LANES, (pair + 1) * LANES)
            later = later_ref[pair]
            pv, tot = _sb_tile(qs[pair], k_ref[rows, lanes], v_ref[rows, lanes], later, None)
            acc_ref[pair] += pv
            later = later + tot
            later_ref[pair] = later
            worst = later if worst is None else jnp.maximum(worst, later)
        return j - 1, jnp.max(worst) > LOG_SURVIVAL_FLOOR

    more = jnp.max(jnp.maximum(live[0], live[1])) > LOG_SURVIVAL_FLOOR
    lax.while_loop(cond, body, (i - 2, more))

    for pair in range(N_PAIR_SB):
        lanes = slice(pair * LANES, (pair + 1) * LANES)
        o = _unstack_heads(acc_ref[pair])
        o_ref[:, lanes] = (o * gate_ref[:, lanes].astype(jnp.float32)).astype(o_ref.dtype)


def _sb_attn(qa, ka, va, gate):
    s = qa.shape[0]
    blk = lambda i: (i, 0)
    resident = pl.BlockSpec((s, W_SB), lambda i: (0, 0), pipeline_mode=pl.Buffered(1))
    return pl.pallas_call(
        _sb_attn_kernel,
        grid=(s // BLOCK,),
        in_specs=[pl.BlockSpec((BLOCK, W_SB), blk), resident, resident,
                  pl.BlockSpec((BLOCK, W_SB), blk)],
        out_specs=pl.BlockSpec((BLOCK, W_SB), blk),
        out_shape=jax.ShapeDtypeStruct((s, W_SB), jnp.bfloat16),
        scratch_shapes=[pltpu.VMEM((N_PAIR_SB, 2 * BLOCK, LANES), jnp.float32),
                        pltpu.VMEM((N_PAIR_SB, 2 * BLOCK, LANES), jnp.float32)],
        compiler_params=pltpu.CompilerParams(
            dimension_semantics=("arbitrary",), vmem_limit_bytes=VMEM_LIMIT_BYTES),
        name="sb_attn",
    )(qa, ka, va, gate)


def _band_attend(q, k_prev, k_own, v_prev, v_own, has_prev):
    qs = _stack_heads(q)
    kk = jnp.concatenate([k_prev, k_own], axis=0)
    vv = jnp.concatenate([v_prev, v_own], axis=0)
    s = lax.dot_general(qs, kk, (((1,), (1,)), ((), ())),
                        preferred_element_type=jnp.float32)
    i = _row_iota(s.shape) % BLOCK
    j = _lane_iota(s.shape)
    band = jnp.logical_and(j >= i, j <= i + BLOCK)
    band = jnp.logical_and(band, jnp.logical_or(j >= BLOCK, has_prev))
    s = jnp.where(band, s, NEG_BIG)
    m = jnp.max(s, axis=-1, keepdims=True)
    p = jnp.exp(s - m)
    l = jnp.sum(p, axis=-1, keepdims=True)
    o = jnp.dot(p.astype(jnp.bfloat16), vv, preferred_element_type=jnp.float32) / l
    log_den = jnp.broadcast_to(m + jnp.log(l), o.shape)
    return _unstack_heads(o), _unstack_heads(log_den)


def _dil_attn_kernel(q1_ref, k1_ref, v1_ref, k1p_ref, v1p_ref,
                     q4_ref, k4_ref, v4_ref, k4p_ref, v4p_ref,
                     q16_ref, k16_ref, v16_ref, k16p_ref, v16p_ref,
                     gate_ref, o_ref, out_s, den_s):
    not_first_unit = pl.program_id(0) > 0
    blocks_per_unit = UNIT // BLOCK

    per_class4 = blocks_per_unit // 4

    def block_rows(n):
        cur = pl.ds(pl.multiple_of(n * BLOCK, BLOCK), BLOCK)
        prv = pl.ds(pl.multiple_of(jnp.maximum(n - 1, 0) * BLOCK, BLOCK), BLOCK)
        return cur, prv

    def body(t, carry):
        cur, prv = block_rows(t)
        head = t == 0
        o, d = _band_attend(q1_ref[cur, :],
                            jnp.where(head, k1p_ref[...], k1_ref[prv, :]), k1_ref[cur, :],
                            jnp.where(head, v1p_ref[...], v1_ref[prv, :]), v1_ref[cur, :],
                            jnp.logical_or(not_first_unit, t > 0))
        out_s[0, cur, :] = o
        den_s[0, cur, :] = d

        cls, n = t // per_class4, t % per_class4
        cur, prv = block_rows(n)
        head = n == 0
        o, d = _band_attend(q4_ref[cls, cur, :],
                            jnp.where(head, k4p_ref[cls], k4_ref[cls, prv, :]), k4_ref[cls, cur, :],
                            jnp.where(head, v4p_ref[cls], v4_ref[cls, prv, :]), v4_ref[cls, cur, :],
                            jnp.logical_or(not_first_unit, n > 0))
        rows = pl.ds(cls + 4 * BLOCK * n, BLOCK, stride=4)
        out_s[1, rows, :] = o
        den_s[1, rows, :] = d

        o, d = _band_attend(q16_ref[t], k16p_ref[t], k16_ref[t], v16p_ref[t], v16_ref[t],
                            not_first_unit)
        rows = pl.ds(t, BLOCK, stride=16)
        out_s[2, rows, :] = o
        den_s[2, rows, :] = d
        return carry

    lax.fori_loop(0, blocks_per_unit, body, 0, unroll=2)

    chunk = 256
    for c0 in range(0, UNIT, chunk):
        rows = slice(c0, c0 + chunk)
        d0, d1, d2 = den_s[0, rows, :], den_s[1, rows, :], den_s[2, rows, :]
        mx = jnp.maximum(jnp.maximum(d0, d1), d2)
        w0, w1, w2 = jnp.exp(d0 - mx), jnp.exp(d1 - mx), jnp.exp(d2 - mx)
        mix = (w0 * out_s[0, rows, :] + w1 * out_s[1, rows, :] + w2 * out_s[2, rows, :]) / (w0 + w1 + w2)
        o_ref[rows, :] = (mix * gate_ref[rows, :].astype(jnp.float32)).astype(o_ref.dtype)


def _dil_attn(q1, k1, v1, q4, k4, v4, q16, k16, v16, gate):
    s = q1.shape[1]
    n_units = s // UNIT
    blocks_per_unit = UNIT // BLOCK
    n4 = UNIT // 4
    prev_unit = lambda u: jnp.maximum(u - 1, 0)
    nat = pl.BlockSpec((None, UNIT, LANES), lambda u, p: (p, u, 0))
    nat_prev = pl.BlockSpec((None, BLOCK, LANES),
                            lambda u, p: (p, jnp.maximum(u * blocks_per_unit - 1, 0), 0))
    c4 = pl.BlockSpec((None, None, 4, n4, LANES), lambda u, p: (p, u, 0, 0, 0))
    c4_prev = pl.BlockSpec((None, None, 4, BLOCK, LANES),
                           lambda u, p: (p, prev_unit(u), 0, n4 // BLOCK - 1, 0))
    c16 = pl.BlockSpec((None, None, 16, BLOCK, LANES), lambda u, p: (p, u, 0, 0, 0))
    c16_prev = pl.BlockSpec((None, None, 16, BLOCK, LANES), lambda u, p: (p, prev_unit(u), 0, 0, 0))
    return pl.pallas_call(
        _dil_attn_kernel,
        grid=(n_units, N_PAIR_DIL),
        in_specs=[nat, nat, nat, nat_prev, nat_prev,
                  c4, c4, c4, c4_prev, c4_prev,
                  c16, c16, c16, c16_prev, c16_prev,
                  pl.BlockSpec((UNIT, LANES), lambda u, p: (u, N_PAIR_SB + p))],
        out_specs=pl.BlockSpec((UNIT, LANES), lambda u, p: (u, p)),
        out_shape=jax.ShapeDtypeStruct((s, W_DIL), jnp.bfloat16),
        scratch_shapes=[pltpu.VMEM((3, UNIT, LANES), jnp.float32),
                        pltpu.VMEM((3, UNIT, LANES), jnp.float32)],
        compiler_params=pltpu.CompilerParams(
            dimension_semantics=("arbitrary", "arbitrary"), vmem_limit_bytes=VMEM_LIMIT_BYTES),
        name="dil_attn",
    )(q1, k1, v1, k1, v1, q4, k4, v4, k4, v4, q16, k16, v16, k16, v16, gate)


def _out_proj_kernel(x_ref, ma_ref, md_ref, w_ref, o_ref):
    y = jnp.dot(ma_ref[...], w_ref[0:W_SB, :], preferred_element_type=jnp.float32)
    y = y + jnp.dot(md_ref[...], w_ref[W_SB:, :], preferred_element_type=jnp.float32)
    o_ref[...] = x_ref[...] + y


def _out_proj(x, mix_a, mix_d, w_bf16):
    s, d = x.shape
    tm = ROW_TILE
    row = lambda i: (i, 0)
    return pl.pallas_call(
        _out_proj_kernel,
        grid=(s // tm,),
        in_specs=[pl.BlockSpec((tm, d), row), pl.BlockSpec((tm, W_SB), row),
                  pl.BlockSpec((tm, W_DIL), row), pl.BlockSpec(w_bf16.shape, lambda i: (0, 0))],
        out_specs=pl.BlockSpec((tm, d), row),
        out_shape=jax.ShapeDtypeStruct((s, d), x.dtype),
        compiler_params=pltpu.CompilerParams(
            dimension_semantics=("arbitrary",), vmem_limit_bytes=VMEM_LIMIT_BYTES),
        name="out_proj",
    )(x, mix_a, mix_d, w_bf16)


def _rope_tables(s):
    inv_freq = 1.0 / (ROPE_THETA ** (jnp.arange(ROPE_HALF, dtype=jnp.float32) * 2.0 / ROPE_DIM))
    ang = jnp.arange(s).astype(jnp.float32)[:, None] * inv_freq[None, :]
    cos, sin = jnp.cos(ang), jnp.sin(ang)
    rest = HEAD_DIM - ROPE_DIM
    zeros = lambda n: jnp.zeros((s, n), jnp.float32)
    cos_t = jnp.concatenate([cos, cos, jnp.ones((s, rest), jnp.float32)], axis=-1)
    sin_a = jnp.concatenate([-sin, zeros(HEAD_DIM - ROPE_HALF)], axis=-1)
    sin_b = jnp.concatenate([zeros(ROPE_HALF), sin, zeros(rest)], axis=-1)
    two = lambda t: jnp.tile(t, (1, LANES // HEAD_DIM))
    return two(cos_t), two(sin_a), two(sin_b)


def kernel(x, norm_g, w_in, q_norm_g, k_norm_g, w_out):
    b, s, d = x.shape
    assert s % UNIT == 0 and d == W_SB + W_DIL
    depth = w_in.shape[0]
    cos_t, sin_a, sin_b = _rope_tables(s)
    w_in_bf = w_in.astype(jnp.bfloat16)
    w_out_bf = w_out.astype(jnp.bfloat16)
    two = lambda g: jnp.tile(g.reshape(1, HEAD_DIM), (1, LANES // HEAD_DIM))
    outs = []
    for bi in range(b):
        xb = x[bi]
        for layer in range(depth):
            (qa, ka, va, gate, q1, k1, v1, q4, k4, v4, q16, k16, v16) = _in_proj(
                xb, norm_g[layer].reshape(1, d), w_in_bf[layer],
                two(q_norm_g[layer]), two(k_norm_g[layer]), cos_t, sin_a, sin_b)
            mix_a = _sb_attn(qa, ka, va, gate)
            mix_d = _dil_attn(q1, k1, v1, q4, k4, v4, q16, k16, v16, gate)
            xb = _out_proj(xb, mix_a, mix_d, w_out_bf[layer])
        outs.append(xb)
    return jnp.stack(outs, axis=0)
```

```python
import math

import jax
import jax.numpy as jnp
import numpy as np
from jax import lax
from jax.experimental import pallas as pl
from jax.experimental.pallas import tpu as pltpu

HEAD_DIM = 64
N_HEADS_SB = 4
N_HEADS_DIL = 12
W_SB = N_HEADS_SB * HEAD_DIM
W_DIL = N_HEADS_DIL * HEAD_DIM
DIL_PATTERNS = ((128, 1), (512, 4), (2048, 16))
ROPE_THETA = 500000.0
ROPE_DIM = HEAD_DIM // 4
ROPE_HALF = ROPE_DIM // 2
BLOCK = 128
EPS = 1e-6
QK_SCALE = 1.0 / math.sqrt(HEAD_DIM)

LANES = 128
UNIT = 16 * BLOCK
N_PAIR_DIL = W_DIL // LANES
N_PAIR_SB = W_SB // LANES
ROW_TILE = 512
VMEM_LIMIT_BYTES = 56 * 1024 * 1024

LOG_SURVIVAL_FLOOR = -90.0
NEG_BIG = -1e30


def _split_bf16(x):
    hi = x.astype(jnp.bfloat16)
    lo = (x - hi.astype(jnp.float32)).astype(jnp.bfloat16)
    return hi, lo


def _lane_iota(shape):
    return lax.broadcasted_iota(jnp.int32, shape, 1)


def _row_iota(shape):
    return lax.broadcasted_iota(jnp.int32, shape, 0)


def _silu(x):
    return x * (1.0 / (1.0 + jnp.exp(-x)))


def _stack_heads(t):
    lane = _lane_iota(t.shape)
    zero = jnp.zeros_like(t)
    return jnp.concatenate([jnp.where(lane < HEAD_DIM, t, zero),
                            jnp.where(lane >= HEAD_DIM, t, zero)], axis=0)


def _unstack_heads(t):
    n = t.shape[0] // 2
    top, bot = t[:n], t[n:]
    return jnp.where(_lane_iota(top.shape) < HEAD_DIM, top, bot)


def _in_proj_kernel(x_ref, g_ref, w_ref, qg_ref, kg_ref, cos_ref, sa_ref, sb_ref,
                    qa_ref, ka_ref, va_ref, gate_ref,
                    q1_ref, k1_ref, v1_ref, q4_ref, k4_ref, v4_ref, q16_ref, k16_ref, v16_ref,
                    slab_ref, slab4_ref):
    tm = x_ref.shape[0]
    x = x_ref[...]
    ms = jnp.mean(x * x, axis=-1, keepdims=True)
    h = (x * lax.rsqrt(ms + EPS) * g_ref[...]).astype(jnp.bfloat16)

    def proj(c0, width):
        return jnp.dot(h, w_ref[:, c0:c0 + width], preferred_element_type=jnp.float32)

    pa = proj(0, 4 * W_SB)
    qa_ref[...] = (pa[:, 0:W_SB] * QK_SCALE).astype(jnp.bfloat16)
    ka_ref[...] = pa[:, W_SB:2 * W_SB].astype(jnp.bfloat16)
    va_ref[...] = pa[:, 2 * W_SB:3 * W_SB].astype(jnp.bfloat16)
    gate_ref[:, 0:W_SB] = _silu(pa[:, 3 * W_SB:]).astype(jnp.bfloat16)
    base = 4 * W_SB
    gate_ref[:, W_SB:] = _silu(proj(base + 3 * W_DIL, W_DIL)).astype(jnp.bfloat16)

    r = (_row_iota((2 * LANES, LANES)) % LANES) // HEAD_DIM
    c = _lane_iota((2 * LANES, LANES)) // HEAD_DIM
    head_mean = jnp.where(r == c, 1.0 / HEAD_DIM, 0.0).astype(jnp.bfloat16)

    cos = cos_ref[...]
    sin_a = sa_ref[...]
    sin_b = sb_ref[...]

    def emit(pair, t, o1_ref, o4_ref, o16_ref):
        o1_ref[pair] = t.astype(jnp.bfloat16)
        slab_ref[pair] = t
        n4 = tm // 4
        for c4 in range(4):
            t4 = slab_ref[pair, pl.ds(c4, n4, stride=4), :]
            o4_ref[pair, c4] = t4.astype(jnp.bfloat16)
            slab4_ref[pair, c4 * n4:(c4 + 1) * n4, :] = t4
        n16 = tm // 16
        for c4 in range(4):
            for b in range(4):
                t16 = slab4_ref[pair, pl.ds(c4 * n4 + b, n16, stride=4), :]
                o16_ref[pair, c4 + 4 * b] = t16.astype(jnp.bfloat16)

    def qk_norm_rope(t, gain, scale):
        hi, lo = _split_bf16(t * t)
        msq = jnp.dot(jnp.concatenate([hi, lo], axis=1), head_mean,
                      preferred_element_type=jnp.float32)
        t = t * lax.rsqrt(msq + EPS) * gain
        up = pltpu.roll(t, LANES - ROPE_HALF, axis=1)
        down = pltpu.roll(t, ROPE_HALF, axis=1)
        t = t * cos + up * sin_a + down * sin_b
        if scale != 1.0:
            t = t * scale
        return t

    pq = proj(base, W_DIL)
    for pair in range(N_PAIR_DIL):
        t = qk_norm_rope(pq[:, pair * LANES:(pair + 1) * LANES], qg_ref[...], QK_SCALE)
        emit(pair, t, q1_ref, q4_ref, q16_ref)
    pk = proj(base + W_DIL, W_DIL)
    for pair in range(N_PAIR_DIL):
        t = qk_norm_rope(pk[:, pair * LANES:(pair + 1) * LANES], kg_ref[...], 1.0)
        emit(pair, t, k1_ref, k4_ref, k16_ref)
    pv = proj(base + 2 * W_DIL, W_DIL)
    for pair in range(N_PAIR_DIL):
        emit(pair, pv[:, pair * LANES:(pair + 1) * LANES], v1_ref, v4_ref, v16_ref)


def _in_proj(x, g, w_bf16, qg, kg, cos_t, sin_a, sin_b):
    s, d = x.shape
    tm = ROW_TILE
    per_unit = UNIT // tm
    n_units = s // UNIT
    bf = jnp.bfloat16
    row = lambda i: (i, 0)
    const = lambda i: (0, 0)
    nat_spec = pl.BlockSpec((N_PAIR_DIL, tm, LANES), lambda i: (0, i, 0))
    c4_spec = pl.BlockSpec((N_PAIR_DIL, None, 4, tm // 4, LANES),
                           lambda i: (0, i // per_unit, 0, i % per_unit, 0))
    c16_spec = pl.BlockSpec((N_PAIR_DIL, None, 16, tm // 16, LANES),
                            lambda i: (0, i // per_unit, 0, i % per_unit, 0))
    nat_shape = jax.ShapeDtypeStruct((N_PAIR_DIL, s, LANES), bf)
    c4_shape = jax.ShapeDtypeStruct((N_PAIR_DIL, n_units, 4, UNIT // 4, LANES), bf)
    c16_shape = jax.ShapeDtypeStruct((N_PAIR_DIL, n_units, 16, UNIT // 16, LANES), bf)
    sb_shape = jax.ShapeDtypeStruct((s, W_SB), bf)
    return pl.pallas_call(
        _in_proj_kernel,
        grid=(s // tm,),
        in_specs=[
            pl.BlockSpec((tm, d), row),
            pl.BlockSpec((1, d), const),
            pl.BlockSpec(w_bf16.shape, const),
            pl.BlockSpec((1, LANES), const),
            pl.BlockSpec((1, LANES), const),
            pl.BlockSpec((tm, LANES), row),
            pl.BlockSpec((tm, LANES), row),
            pl.BlockSpec((tm, LANES), row),
        ],
        out_specs=[
            pl.BlockSpec((tm, W_SB), row), pl.BlockSpec((tm, W_SB), row),
            pl.BlockSpec((tm, W_SB), row), pl.BlockSpec((tm, W_SB + W_DIL), row),
            nat_spec, nat_spec, nat_spec, c4_spec, c4_spec, c4_spec,
            c16_spec, c16_spec, c16_spec,
        ],
        out_shape=[sb_shape, sb_shape, sb_shape,
                   jax.ShapeDtypeStruct((s, W_SB + W_DIL), bf),
                   nat_shape, nat_shape, nat_shape, c4_shape, c4_shape, c4_shape,
                   c16_shape, c16_shape, c16_shape],
        scratch_shapes=[pltpu.VMEM((N_PAIR_DIL, tm, LANES), jnp.float32),
                        pltpu.VMEM((N_PAIR_DIL, tm, LANES), jnp.float32)],
        compiler_params=pltpu.CompilerParams(
            dimension_semantics=("arbitrary",), vmem_limit_bytes=VMEM_LIMIT_BYTES),
        name="in_proj",
    )(x, g, w_bf16, qg, kg, cos_t, sin_a, sin_b)


def _spread(per_head):
    return jnp.concatenate([jnp.broadcast_to(t, (BLOCK, BLOCK)) for t in per_head], axis=1)


def _sb_tile(q, kb, vb, later, mask, tri2):
    z = lax.dot_general(q, _stack_heads(kb), (((1,), (1,)), ((), ())),
                        preferred_element_type=jnp.float32)
    soft = jnp.log(1.0 + jnp.exp(-jnp.abs(z)))
    log_surv = -(jnp.maximum(z, 0.0) + soft)
    log_beta = jnp.minimum(z, 0.0) - soft
    if mask is not None:
        log_surv = jnp.where(mask, log_surv, 0.0)
    hi, lo = _split_bf16(log_surv)
    w = jnp.dot(jnp.concatenate([hi, lo], axis=0), tri2, preferred_element_type=jnp.float32)
    within = w[:BLOCK] + w[BLOCK:]
    a = jnp.exp(log_beta + within + later)
    if mask is not None:
        a = jnp.where(mask, a, 0.0)
    pv = jnp.dot(a.astype(jnp.bfloat16), _stack_heads(vb), preferred_element_type=jnp.float32)
    tot = [jnp.sum(log_surv[:, hd * BLOCK:(hd + 1) * BLOCK], axis=-1, keepdims=True)
           for hd in range(2)]
    return pv, tot


def _sb_attn_kernel(q_ref, k_ref, v_ref, gate_ref, tri2_ref, o_ref, acc_ref, later_ref):
    i = pl.program_id(0)
    tri2 = tri2_ref[...]
    shape2 = (BLOCK, 2 * BLOCK)
    strictly_before = _lane_iota(shape2) % BLOCK < _row_iota(shape2)
    diag = pl.ds(pl.multiple_of(i * BLOCK, BLOCK), BLOCK)
    prev = pl.ds(pl.multiple_of(jnp.maximum(i - 1, 0) * BLOCK, BLOCK), BLOCK)
    no_prev = jnp.where(i > 0, 0.0, NEG_BIG)

    qs, worst = [], None
    for pair in range(N_PAIR_SB):
        lanes = slice(pair * LANES, (pair + 1) * LANES)
        q = q_ref[:, lanes]
        pv0, tot0 = _sb_tile(q, k_ref[diag, lanes], v_ref[diag, lanes], 0.0, strictly_before, tri2)
        pv1, tot1 = _sb_tile(q, k_ref[prev, lanes], v_ref[prev, lanes],
                             _spread(tot0) + no_prev, None, tri2)
        later = _spread([tot0[0] + tot1[0], tot0[1] + tot1[1]])
        acc_ref[pair] = pv0 + pv1
        later_ref[pair] = later
        qs.append(q)
        worst = later if worst is None else jnp.maximum(worst, later)

    def cond(carry):
        j, more = carry
        return jnp.logical_and(j >= 0, more)

    def body(carry):
        j, _ = carry
        rows = pl.ds(pl.multiple_of(j * BLOCK, BLOCK), BLOCK)
        worst = None
        for pair in range(N_PAIR_SB):
            lanes = slice(pair * LANES, (pair + 1) * LANES)
            later = later_ref[pair]
            pv, tot = _sb_tile(qs[pair], k_ref[rows, lanes], v_ref[rows, lanes], later, None, tri2)
            acc_ref[pair] += pv
            later = later + _spread(tot)
            later_ref[pair] = later
            worst = later if worst is None else jnp.maximum(worst, later)
        return j - 1, jnp.max(worst) > LOG_SURVIVAL_FLOOR

    lax.while_loop(cond, body, (i - 2, jnp.max(worst) > LOG_SURVIVAL_FLOOR))

    for pair in range(N_PAIR_SB):
        lanes = slice(pair * LANES, (pair + 1) * LANES)
        o_ref[:, lanes] = (acc_ref[pair] * gate_ref[:, lanes].astype(jnp.float32)).astype(o_ref.dtype)


def _sb_attn(qa, ka, va, gate):
    s = qa.shape[0]
    blk = lambda i: (i, 0)
    resident = pl.BlockSpec((s, W_SB), lambda i: (0, 0), pipeline_mode=pl.Buffered(1))
    idx = np.arange(2 * BLOCK)
    tri2 = jnp.asarray((idx[:, None] > idx[None, :]) & (idx[:, None] // BLOCK == idx[None, :] // BLOCK),
                       dtype=jnp.bfloat16)
    return pl.pallas_call(
        _sb_attn_kernel,
        grid=(s // BLOCK,),
        in_specs=[pl.BlockSpec((BLOCK, W_SB), blk), resident, resident,
                  pl.BlockSpec((BLOCK, W_SB), blk),
                  pl.BlockSpec(tri2.shape, lambda i: (0, 0))],
        out_specs=pl.BlockSpec((BLOCK, W_SB), blk),
        out_shape=jax.ShapeDtypeStruct((s, W_SB), jnp.bfloat16),
        scratch_shapes=[pltpu.VMEM((N_PAIR_SB, BLOCK, LANES), jnp.float32),
                        pltpu.VMEM((N_PAIR_SB, BLOCK, 2 * BLOCK), jnp.float32)],
        compiler_params=pltpu.CompilerParams(
            dimension_semantics=("arbitrary",), vmem_limit_bytes=VMEM_LIMIT_BYTES),
        name="sb_attn",
    )(qa, ka, va, gate, tri2)


def _band_scores(q, k_prev, k_own, bias):
    kk = jnp.concatenate([k_prev, k_own], axis=0)
    return lax.dot_general(_stack_heads(q), kk, (((1,), (1,)), ((), ())),
                           preferred_element_type=jnp.float32) + bias


def _band_softmax_pv(s, v_prev, v_own):
    vv = jnp.concatenate([v_prev, v_own], axis=0)
    m = jnp.max(s, axis=-1, keepdims=True)
    p = jnp.exp(s - m)
    l = jnp.sum(p, axis=-1, keepdims=True)
    u = jnp.dot(p.astype(jnp.bfloat16), vv, preferred_element_type=jnp.float32)
    return (_unstack_heads(u), _unstack_heads(jnp.broadcast_to(m, u.shape)),
            _unstack_heads(jnp.broadcast_to(l, u.shape)))


def _dil_attn_kernel(q1_ref, k1_ref, v1_ref, k1p_ref, v1p_ref,
                     q4_ref, k4_ref, v4_ref, k4p_ref, v4p_ref,
                     q16_ref, k16_ref, v16_ref, k16p_ref, v16p_ref,
                     gate_ref, bias_ref, o_ref, out_s, max_s, den_s, score_s):
    not_first_unit = pl.program_id(0) > 0
    blocks_per_unit = UNIT // BLOCK
    per_class4 = blocks_per_unit // 4

    def block_rows(n):
        cur = pl.ds(pl.multiple_of(n * BLOCK, BLOCK), BLOCK)
        prv = pl.ds(pl.multiple_of(jnp.maximum(n - 1, 0) * BLOCK, BLOCK), BLOCK)
        return cur, prv

    def band_bias(has_prev):
        return bias_ref[has_prev.astype(jnp.int32)]

    def scores(t, slot):
        cur, prv = block_rows(t)
        score_s[slot, 0] = _band_scores(
            q1_ref[cur, :], jnp.where(t == 0, k1p_ref[...], k1_ref[prv, :]), k1_ref[cur, :],
            band_bias(jnp.logical_or(not_first_unit, t > 0)))
        cls, n = t // per_class4, t % per_class4
        cur, prv = block_rows(n)
        score_s[slot, 1] = _band_scores(
            q4_ref[cls, cur, :], jnp.where(n == 0, k4p_ref[cls], k4_ref[cls, prv, :]),
            k4_ref[cls, cur, :], band_bias(jnp.logical_or(not_first_unit, n > 0)))
        score_s[slot, 2] = _band_scores(q16_ref[t], k16p_ref[t], k16_ref[t],
                                        band_bias(not_first_unit))

    def put(pattern, rows, res):
        u, m, l = res
        out_s[pattern, rows, :] = u
        max_s[pattern, rows, :] = m
        den_s[pattern, rows, :] = l

    def softmax_pv(t, slot):
        cur, prv = block_rows(t)
        put(0, cur, _band_softmax_pv(
            score_s[slot, 0], jnp.where(t == 0, v1p_ref[...], v1_ref[prv, :]), v1_ref[cur, :]))
        cls, n = t // per_class4, t % per_class4
        cur, prv = block_rows(n)
        put(1, pl.ds(cls + 4 * BLOCK * n, BLOCK, stride=4), _band_softmax_pv(
            score_s[slot, 1], jnp.where(n == 0, v4p_ref[cls], v4_ref[cls, prv, :]),
            v4_ref[cls, cur, :]))
        put(2, pl.ds(t, BLOCK, stride=16), _band_softmax_pv(
            score_s[slot, 2], v16p_ref[t], v16_ref[t]))

    scores(0, 0)

    def body(k, carry):
        t = 2 * k
        scores(t + 1, 1)
        softmax_pv(t, 0)
        scores(jnp.minimum(t + 2, blocks_per_unit - 1), 0)
        softmax_pv(t + 1, 1)
        return carry

    lax.fori_loop(0, blocks_per_unit // 2, body, 0)

    chunk = 256
    for c0 in range(0, UNIT, chunk):
        rows = slice(c0, c0 + chunk)
        m0, m1, m2 = max_s[0, rows, :], max_s[1, rows, :], max_s[2, rows, :]
        mx = jnp.maximum(jnp.maximum(m0, m1), m2)
        w0, w1, w2 = jnp.exp(m0 - mx), jnp.exp(m1 - mx), jnp.exp(m2 - mx)
        num = w0 * out_s[0, rows, :] + w1 * out_s[1, rows, :] + w2 * out_s[2, rows, :]
        den = w0 * den_s[0, rows, :] + w1 * den_s[1, rows, :] + w2 * den_s[2, rows, :]
        o_ref[rows, :] = (num / den * gate_ref[rows, :].astype(jnp.float32)).astype(o_ref.dtype)


def _band_bias_table():
    i = np.arange(2 * BLOCK)[:, None] % BLOCK
    j = np.arange(2 * BLOCK)[None, :]
    band = (j >= i) & (j <= i + BLOCK)
    table = np.stack([band & (j >= BLOCK), band])
    return jnp.asarray(np.where(table, 0.0, NEG_BIG), dtype=jnp.float32)


def _dil_attn(q1, k1, v1, q4, k4, v4, q16, k16, v16, gate):
    s = q1.shape[1]
    n_units = s // UNIT
    blocks_per_unit = UNIT // BLOCK
    n4 = UNIT // 4
    prev_unit = lambda u: jnp.maximum(u - 1, 0)
    nat = pl.BlockSpec((None, UNIT, LANES), lambda u, p: (p, u, 0))
    nat_prev = pl.BlockSpec((None, BLOCK, LANES),
                            lambda u, p: (p, jnp.maximum(u * blocks_per_unit - 1, 0), 0))
    c4 = pl.BlockSpec((None, None, 4, n4, LANES), lambda u, p: (p, u, 0, 0, 0))
    c4_prev = pl.BlockSpec((None, None, 4, BLOCK, LANES),
                           lambda u, p: (p, prev_unit(u), 0, n4 // BLOCK - 1, 0))
    c16 = pl.BlockSpec((None, None, 16, BLOCK, LANES), lambda u, p: (p, u, 0, 0, 0))
    c16_prev = pl.BlockSpec((None, None, 16, BLOCK, LANES), lambda u, p: (p, prev_unit(u), 0, 0, 0))
    bias = _band_bias_table()
    return pl.pallas_call(
        _dil_attn_kernel,
        grid=(n_units, N_PAIR_DIL),
        in_specs=[nat, nat, nat, nat_prev, nat_prev,
                  c4, c4, c4, c4_prev, c4_prev,
                  c16, c16, c16, c16_prev, c16_prev,
                  pl.BlockSpec((UNIT, LANES), lambda u, p: (u, N_PAIR_SB + p)),
                  pl.BlockSpec(bias.shape, lambda u, p: (0, 0, 0))],
        out_specs=pl.BlockSpec((UNIT, LANES), lambda u, p: (u, p)),
        out_shape=jax.ShapeDtypeStruct((s, W_DIL), jnp.bfloat16),
        scratch_shapes=[pltpu.VMEM((3, UNIT, LANES), jnp.float32),
                        pltpu.VMEM((3, UNIT, LANES), jnp.float32),
                        pltpu.VMEM((3, UNIT, LANES), jnp.float32),
                        pltpu.VMEM((2, 3, 2 * BLOCK, 2 * BLOCK), jnp.float32)],
        compiler_params=pltpu.CompilerParams(
            dimension_semantics=("arbitrary", "arbitrary"), vmem_limit_bytes=VMEM_LIMIT_BYTES),
        name="dil_attn",
    )(q1, k1, v1, k1, v1, q4, k4, v4, k4, v4, q16, k16, v16, k16, v16, gate, bias)


def _out_proj_kernel(x_ref, ma_ref, md_ref, w_ref, o_ref):
    y = jnp.dot(ma_ref[...], w_ref[0:W_SB, :], preferred_element_type=jnp.float32)
    y = y + jnp.dot(md_ref[...], w_ref[W_SB:, :], preferred_element_type=jnp.float32)
    o_ref[...] = x_ref[...] + y


def _out_proj(x, mix_a, mix_d, w_bf16):
    s, d = x.shape
    tm = ROW_TILE
    row = lambda i: (i, 0)
    return pl.pallas_call(
        _out_proj_kernel,
        grid=(s // tm,),
        in_specs=[pl.BlockSpec((tm, d), row), pl.BlockSpec((tm, W_SB), row),
                  pl.BlockSpec((tm, W_DIL), row), pl.BlockSpec(w_bf16.shape, lambda i: (0, 0))],
        out_specs=pl.BlockSpec((tm, d), row),
        out_shape=jax.ShapeDtypeStruct((s, d), x.dtype),
        compiler_params=pltpu.CompilerParams(
            dimension_semantics=("arbitrary",), vmem_limit_bytes=VMEM_LIMIT_BYTES),
        name="out_proj",
    )(x, mix_a, mix_d, w_bf16)


def _rope_tables(s):
    inv_freq = 1.0 / (ROPE_THETA ** (jnp.arange(ROPE_HALF, dtype=jnp.float32) * 2.0 / ROPE_DIM))
    ang = jnp.arange(s).astype(jnp.float32)[:, None] * inv_freq[None, :]
    cos, sin = jnp.cos(ang), jnp.sin(ang)
    rest = HEAD_DIM - ROPE_DIM
    zeros = lambda n: jnp.zeros((s, n), jnp.float32)
    cos_t = jnp.concatenate([cos, cos, jnp.ones((s, rest), jnp.float32)], axis=-1)
    sin_a = jnp.concatenate([-sin, zeros(HEAD_DIM - ROPE_HALF)], axis=-1)
    sin_b = jnp.concatenate([zeros(ROPE_HALF), sin, zeros(rest)], axis=-1)
    two = lambda t: jnp.tile(t, (1, LANES // HEAD_DIM))
    return two(cos_t), two(sin_a), two(sin_b)


def kernel(x, norm_g, w_in, q_norm_g, k_norm_g, w_out):
    b, s, d = x.shape
    assert s % UNIT == 0 and d == W_SB + W_DIL
    depth = w_in.shape[0]
    cos_t, sin_a, sin_b = _rope_tables(s)
    w_in_bf = w_in.astype(jnp.bfloat16)
    w_out_bf = w_out.astype(jnp.bfloat16)
    two = lambda g: jnp.tile(g.reshape(1, HEAD_DIM), (1, LANES // HEAD_DIM))
    outs = []
    for bi in range(b):
        xb = x[bi]
        for layer in range(depth):
            (qa, ka, va, gate, q1, k1, v1, q4, k4, v4, q16, k16, v16) = _in_proj(
                xb, norm_g[layer].reshape(1, d), w_in_bf[layer],
                two(q_norm_g[layer]), two(k_norm_g[layer]), cos_t, sin_a, sin_b)
            mix_a = _sb_attn(qa, ka, va, gate)
            mix_d = _dil_attn(q1, k1, v1, q4, k4, v4, q16, k16, v16, gate)
            xb = _out_proj(xb, mix_a, mix_d, w_out_bf[layer])
        outs.append(xb)
    return jnp.stack(outs, axis=0)
```

```python
import math

import jax
import jax.numpy as jnp
import numpy as np
from jax import lax
from jax.experimental import pallas as pl
from jax.experimental.pallas import tpu as pltpu

HEAD_DIM = 64
N_HEADS_SB = 4
N_HEADS_DIL = 12
W_SB = N_HEADS_SB * HEAD_DIM
W_DIL = N_HEADS_DIL * HEAD_DIM
DIL_PATTERNS = ((128, 1), (512, 4), (2048, 16))
ROPE_THETA = 500000.0
ROPE_DIM = HEAD_DIM // 4
ROPE_HALF = ROPE_DIM // 2
BLOCK = 128
EPS = 1e-6
QK_SCALE = 1.0 / math.sqrt(HEAD_DIM)
LOG2_E = math.log2(math.e)

LANES = 128
UNIT = 16 * BLOCK
N_PAIR_DIL = W_DIL // LANES
N_PAIR_SB = W_SB // LANES
ROW_TILE = 512
SOFTMAX_ROWS = 64
SB_Q_BLOCKS = 2
VMEM_LIMIT_BYTES = 56 * 1024 * 1024

LOG_SURVIVAL_FLOOR = -90.0
NEG_BIG = -1e30


def _split_bf16(x):
    hi = x.astype(jnp.bfloat16)
    lo = (x - hi.astype(jnp.float32)).astype(jnp.bfloat16)
    return hi, lo


def _lane_iota(shape):
    return lax.broadcasted_iota(jnp.int32, shape, 1)


def _row_iota(shape):
    return lax.broadcasted_iota(jnp.int32, shape, 0)


def _silu(x):
    return x * (1.0 / (1.0 + jnp.exp(-x)))


def _stack_heads(t):
    lane = _lane_iota(t.shape)
    zero = jnp.zeros_like(t)
    return jnp.concatenate([jnp.where(lane < HEAD_DIM, t, zero),
                            jnp.where(lane >= HEAD_DIM, t, zero)], axis=0)


def _unstack_heads(t):
    n = t.shape[0] // 2
    top, bot = t[:n], t[n:]
    return jnp.where(_lane_iota(top.shape) < HEAD_DIM, top, bot)


def _in_proj_kernel(x_ref, *refs):
    _in_proj_body(x_ref[...], *refs)


def _out_in_proj_kernel(x_ref, ma_ref, md_ref, wo_ref, *refs):
    (g_ref, w_ref, qg_ref, kg_ref, cos_ref, sa_ref, sb_ref, xn_ref, *out_and_scratch) = refs
    y = jnp.dot(ma_ref[...], wo_ref[0:W_SB, :], preferred_element_type=jnp.float32)
    y = y + jnp.dot(md_ref[...], wo_ref[W_SB:, :], preferred_element_type=jnp.float32)
    x = x_ref[...] + y
    xn_ref[...] = x
    _in_proj_body(x, g_ref, w_ref, qg_ref, kg_ref, cos_ref, sa_ref, sb_ref, *out_and_scratch)


def _in_proj_body(x, g_ref, w_ref, qg_ref, kg_ref, cos_ref, sa_ref, sb_ref,
                  qa_ref, ka_ref, va_ref, gate_ref,
                  q1_ref, k1_ref, v1_ref, q4_ref, k4_ref, v4_ref, q16_ref, k16_ref, v16_ref,
                  slab_ref, slab4_ref):
    tm = x.shape[0]
    ms = jnp.mean(x * x, axis=-1, keepdims=True)
    h = (x * lax.rsqrt(ms + EPS) * g_ref[...]).astype(jnp.bfloat16)

    def proj(c0, width):
        return jnp.dot(h, w_ref[:, c0:c0 + width], preferred_element_type=jnp.float32)

    pa = proj(0, 4 * W_SB)
    qa_ref[...] = (pa[:, 0:W_SB] * QK_SCALE).astype(jnp.bfloat16)
    ka_ref[...] = pa[:, W_SB:2 * W_SB].astype(jnp.bfloat16)
    va_ref[...] = pa[:, 2 * W_SB:3 * W_SB].astype(jnp.bfloat16)
    gate_ref[:, 0:W_SB] = _silu(pa[:, 3 * W_SB:]).astype(jnp.bfloat16)
    base = 4 * W_SB
    gate_ref[:, W_SB:] = _silu(proj(base + 3 * W_DIL, W_DIL)).astype(jnp.bfloat16)

    r = (_row_iota((2 * LANES, LANES)) % LANES) // HEAD_DIM
    c = _lane_iota((2 * LANES, LANES)) // HEAD_DIM
    head_mean = jnp.where(r == c, 1.0 / HEAD_DIM, 0.0).astype(jnp.bfloat16)

    cos = cos_ref[...]
    sin_a = sa_ref[...]
    sin_b = sb_ref[...]

    def emit(pair, t, o1_ref, o4_ref, o16_ref):
        o1_ref[pair] = t.astype(jnp.bfloat16)
        slab_ref[pair] = t
        n4 = tm // 4
        for c4 in range(4):
            t4 = slab_ref[pair, pl.ds(c4, n4, stride=4), :]
            o4_ref[pair, c4] = t4.astype(jnp.bfloat16)
            slab4_ref[pair, c4 * n4:(c4 + 1) * n4, :] = t4
        n16 = tm // 16
        for c4 in range(4):
            for b in range(4):
                t16 = slab4_ref[pair, pl.ds(c4 * n4 + b, n16, stride=4), :]
                o16_ref[pair, c4 + 4 * b] = t16.astype(jnp.bfloat16)

    def qk_norm_rope(t, gain, scale):
        hi, lo = _split_bf16(t * t)
        msq = jnp.dot(jnp.concatenate([hi, lo], axis=1), head_mean,
                      preferred_element_type=jnp.float32)
        t = t * lax.rsqrt(msq + EPS) * gain
        up = pltpu.roll(t, LANES - ROPE_HALF, axis=1)
        down = pltpu.roll(t, ROPE_HALF, axis=1)
        t = t * cos + up * sin_a + down * sin_b
        if scale != 1.0:
            t = t * scale
        return t

    pq = proj(base, W_DIL)
    for pair in range(N_PAIR_DIL):
        t = qk_norm_rope(pq[:, pair * LANES:(pair + 1) * LANES], qg_ref[...], QK_SCALE * LOG2_E)
        emit(pair, t, q1_ref, q4_ref, q16_ref)
    pk = proj(base + W_DIL, W_DIL)
    for pair in range(N_PAIR_DIL):
        t = qk_norm_rope(pk[:, pair * LANES:(pair + 1) * LANES], kg_ref[...], 1.0)
        emit(pair, t, k1_ref, k4_ref, k16_ref)
    pv = proj(base + 2 * W_DIL, W_DIL)
    for pair in range(N_PAIR_DIL):
        emit(pair, pv[:, pair * LANES:(pair + 1) * LANES], v1_ref, v4_ref, v16_ref)


def _in_proj(layer, x, g, w_bf16, qg, kg, cos_t, sin_a, sin_b, prev=None):
    s, d = x.shape
    tm = ROW_TILE
    per_unit = UNIT // tm
    n_units = s // UNIT
    bf = jnp.bfloat16
    row = lambda i: (i, 0)
    nat_spec = pl.BlockSpec((N_PAIR_DIL, tm, LANES), lambda i: (0, i, 0))
    c4_spec = pl.BlockSpec((N_PAIR_DIL, None, 4, tm // 4, LANES),
                           lambda i: (0, i // per_unit, 0, i % per_unit, 0))
    c16_spec = pl.BlockSpec((N_PAIR_DIL, None, 16, tm // 16, LANES),
                            lambda i: (0, i // per_unit, 0, i % per_unit, 0))
    nat_shape = jax.ShapeDtypeStruct((N_PAIR_DIL, s, LANES), bf)
    c4_shape = jax.ShapeDtypeStruct((N_PAIR_DIL, n_units, 4, UNIT // 4, LANES), bf)
    c16_shape = jax.ShapeDtypeStruct((N_PAIR_DIL, n_units, 16, UNIT // 16, LANES), bf)
    sb_shape = jax.ShapeDtypeStruct((s, W_SB), bf)
    in_specs = [
        pl.BlockSpec((None, 1, d), lambda i: (layer, 0, 0)),
        pl.BlockSpec((None,) + w_bf16.shape[1:], lambda i: (layer, 0, 0)),
        pl.BlockSpec((None, 1, LANES), lambda i: (layer, 0, 0)),
        pl.BlockSpec((None, 1, LANES), lambda i: (layer, 0, 0)),
        pl.BlockSpec((tm, LANES), row),
        pl.BlockSpec((tm, LANES), row),
        pl.BlockSpec((tm, LANES), row),
    ]
    operands = [g, w_bf16, qg, kg, cos_t, sin_a, sin_b]
    out_specs = [
        pl.BlockSpec((tm, W_SB), row), pl.BlockSpec((tm, W_SB), row),
        pl.BlockSpec((tm, W_SB), row), pl.BlockSpec((tm, W_SB + W_DIL), row),
        nat_spec, nat_spec, nat_spec, c4_spec, c4_spec, c4_spec,
        c16_spec, c16_spec, c16_spec,
    ]
    out_shape = [sb_shape, sb_shape, sb_shape,
                 jax.ShapeDtypeStruct((s, W_SB + W_DIL), bf),
                 nat_shape, nat_shape, nat_shape, c4_shape, c4_shape, c4_shape,
                 c16_shape, c16_shape, c16_shape]
    x_spec = pl.BlockSpec((tm, d), row)
    if prev is None:
        body, name = _in_proj_kernel, "in_proj"
        in_specs = [x_spec] + in_specs
        operands = [x] + operands
    else:
        mix_a, mix_d, w_out = prev
        body, name = _out_in_proj_kernel, "out_in_proj"
        in_specs = [x_spec, pl.BlockSpec((tm, W_SB), row), pl.BlockSpec((tm, W_DIL), row),
                    pl.BlockSpec((None,) + w_out.shape[1:], lambda i: (layer - 1, 0, 0))] + in_specs
        operands = [x, mix_a, mix_d, w_out] + operands
        out_specs = [x_spec] + out_specs
        out_shape = [jax.ShapeDtypeStruct((s, d), x.dtype)] + out_shape
    return pl.pallas_call(
        body,
        grid=(s // tm,),
        in_specs=in_specs,
        out_specs=out_specs,
        out_shape=out_shape,
        scratch_shapes=[pltpu.VMEM((N_PAIR_DIL, tm, LANES), jnp.float32),
                        pltpu.VMEM((N_PAIR_DIL, tm, LANES), jnp.float32)],
        compiler_params=pltpu.CompilerParams(
            dimension_semantics=("arbitrary",), vmem_limit_bytes=VMEM_LIMIT_BYTES),
        name=name,
    )(*operands)


def _spread(per_head):
    return jnp.concatenate([jnp.broadcast_to(t, (BLOCK, BLOCK)) for t in per_head], axis=1)


def _sb_tile(q, kb, vb, later, mask, tri2):
    z = lax.dot_general(q, _stack_heads(kb), (((1,), (1,)), ((), ())),
                        preferred_element_type=jnp.float32)
    soft = jnp.log(1.0 + jnp.exp(-jnp.abs(z)))
    log_surv = -(jnp.maximum(z, 0.0) + soft)
    log_beta = jnp.minimum(z, 0.0) - soft
    if mask is not None:
        log_surv = jnp.where(mask, log_surv, 0.0)
    hi, lo = _split_bf16(log_surv)
    w = jnp.dot(jnp.concatenate([hi, lo], axis=0), tri2, preferred_element_type=jnp.float32)
    within = w[:BLOCK] + w[BLOCK:]
    a = jnp.exp(log_beta + within + later)
    if mask is not None:
        a = jnp.where(mask, a, 0.0)
    pv = jnp.dot(a.astype(jnp.bfloat16), _stack_heads(vb), preferred_element_type=jnp.float32)
    tot = [jnp.sum(log_surv[:, hd * BLOCK:(hd + 1) * BLOCK], axis=-1, keepdims=True)
           for hd in range(2)]
    return pv, tot


def _sb_attn_kernel(q_ref, k_ref, v_ref, gate_ref, tri2_ref, o_ref, acc_ref, later_ref):
    step = pl.program_id(0)
    tri2 = tri2_ref[...]
    shape2 = (BLOCK, 2 * BLOCK)
    strictly_before = _lane_iota(shape2) % BLOCK < _row_iota(shape2)

    def key_rows(j):
        return pl.ds(pl.multiple_of(j * BLOCK, BLOCK), BLOCK)

    qs, more = [], []
    for sub in range(SB_Q_BLOCKS):
        i = step * SB_Q_BLOCKS + sub
        rows = slice(sub * BLOCK, (sub + 1) * BLOCK)
        diag, prev = key_rows(i), key_rows(jnp.maximum(i - 1, 0))
        no_prev = jnp.where(i > 0, 0.0, NEG_BIG)
        worst = None
        for pair in range(N_PAIR_SB):
            lanes = slice(pair * LANES, (pair + 1) * LANES)
            slot = sub * N_PAIR_SB + pair
            q = q_ref[rows, lanes]
            pv0, tot0 = _sb_tile(q, k_ref[diag, lanes], v_ref[diag, lanes], 0.0,
                                 strictly_before, tri2)
            pv1, tot1 = _sb_tile(q, k_ref[prev, lanes], v_ref[prev, lanes],
                                 _spread(tot0) + no_prev, None, tri2)
            later = _spread([tot0[0] + tot1[0], tot0[1] + tot1[1]])
            acc_ref[slot] = pv0 + pv1
            later_ref[slot] = later
            qs.append(q)
            worst = later if worst is None else jnp.maximum(worst, later)
        more.append(jnp.max(worst) > LOG_SURVIVAL_FLOOR)

    for sub in range(SB_Q_BLOCKS):
        def cond(carry):
            j, live = carry
            return jnp.logical_and(j >= 0, live)

        def body(carry, sub=sub):
            j, _ = carry
            rows = key_rows(j)
            worst = None
            for pair in range(N_PAIR_SB):
                lanes = slice(pair * LANES, (pair + 1) * LANES)
                slot = sub * N_PAIR_SB + pair
                later = later_ref[slot]
                pv, tot = _sb_tile(qs[slot], k_ref[rows, lanes], v_ref[rows, lanes], later, None, tri2)
                acc_ref[slot] += pv
                later = later + _spread(tot)
                later_ref[slot] = later
                worst = later if worst is None else jnp.maximum(worst, later)
            return j - 1, jnp.max(worst) > LOG_SURVIVAL_FLOOR

        lax.while_loop(cond, body, (step * SB_Q_BLOCKS + sub - 2, more[sub]))

    for sub in range(SB_Q_BLOCKS):
        rows = slice(sub * BLOCK, (sub + 1) * BLOCK)
        for pair in range(N_PAIR_SB):
            lanes = slice(pair * LANES, (pair + 1) * LANES)
            gate = gate_ref[rows, lanes].astype(jnp.float32)
            o_ref[rows, lanes] = (acc_ref[sub * N_PAIR_SB + pair] * gate).astype(o_ref.dtype)


def _sb_attn(qa, ka, va, gate):
    s = qa.shape[0]
    tq = SB_Q_BLOCKS * BLOCK
    blk = lambda i: (i, 0)
    resident = pl.BlockSpec((s, W_SB), lambda i: (0, 0), pipeline_mode=pl.Buffered(1))
    idx = np.arange(2 * BLOCK)
    tri2 = jnp.asarray((idx[:, None] > idx[None, :]) & (idx[:, None] // BLOCK == idx[None, :] // BLOCK),
                       dtype=jnp.bfloat16)
    n_slots = SB_Q_BLOCKS * N_PAIR_SB
    return pl.pallas_call(
        _sb_attn_kernel,
        grid=(s // tq,),
        in_specs=[pl.BlockSpec((tq, W_SB), blk), resident, resident,
                  pl.BlockSpec((tq, W_SB), blk),
                  pl.BlockSpec(tri2.shape, lambda i: (0, 0))],
        out_specs=pl.BlockSpec((tq, W_SB), blk),
        out_shape=jax.ShapeDtypeStruct((s, W_SB), jnp.bfloat16),
        scratch_shapes=[pltpu.VMEM((n_slots, BLOCK, LANES), jnp.float32),
                        pltpu.VMEM((n_slots, BLOCK, 2 * BLOCK), jnp.float32)],
        compiler_params=pltpu.CompilerParams(
            dimension_semantics=("arbitrary",), vmem_limit_bytes=VMEM_LIMIT_BYTES),
        name="sb_attn",
    )(qa, ka, va, gate, tri2)


def _band_scores(q, k_prev, k_own, bias):
    kk = jnp.concatenate([k_prev, k_own], axis=0)
    return lax.dot_general(_stack_heads(q), kk, (((1,), (1,)), ((), ())),
                           preferred_element_type=jnp.float32) + bias


def _band_softmax_pv(s_ref, p_ref, v_prev, v_own):
    vv = jnp.concatenate([v_prev, v_own], axis=0)
    ms, ls = [], []
    for r0 in range(0, 2 * BLOCK, SOFTMAX_ROWS):
        rows = slice(r0, r0 + SOFTMAX_ROWS)
        s = s_ref[rows, :]
        m = jnp.max(s, axis=-1, keepdims=True)
        p = jnp.exp2(s - m)
        ms.append(m)
        ls.append(jnp.sum(p, axis=-1, keepdims=True))
        p_ref[rows, :] = p.astype(jnp.bfloat16)
    u = jnp.dot(p_ref[...], vv, preferred_element_type=jnp.float32)

    def unstack_columns(cols):
        half = len(cols) // 2
        head0 = _lane_iota((SOFTMAX_ROWS, LANES)) < HEAD_DIM
        return jnp.concatenate(
            [jnp.where(head0, jnp.broadcast_to(cols[c], (SOFTMAX_ROWS, LANES)),
                       jnp.broadcast_to(cols[half + c], (SOFTMAX_ROWS, LANES)))
             for c in range(half)], axis=0)

    return _unstack_heads(u), unstack_columns(ms), unstack_columns(ls)


def _dil_attn_kernel(q1_ref, k1_ref, v1_ref, k1p_ref, v1p_ref,
                     q4_ref, k4_ref, v4_ref, k4p_ref, v4p_ref,
                     q16_ref, k16_ref, v16_ref, k16p_ref, v16p_ref,
                     gate_ref, bias_ref, o_ref, out_s, max_s, den_s, score_s, prob_s):
    not_first_unit = pl.program_id(0) > 0
    blocks_per_unit = UNIT // BLOCK
    per_class4 = blocks_per_unit // 4

    def block_rows(n):
        cur = pl.ds(pl.multiple_of(n * BLOCK, BLOCK), BLOCK)
        prv = pl.ds(pl.multiple_of(jnp.maximum(n - 1, 0) * BLOCK, BLOCK), BLOCK)
        return cur, prv

    def band_bias(has_prev):
        return bias_ref[has_prev.astype(jnp.int32)]

    def scores(t, slot):
        cur, prv = block_rows(t)
        score_s[slot, 0] = _band_scores(
            q1_ref[cur, :], jnp.where(t == 0, k1p_ref[...], k1_ref[prv, :]), k1_ref[cur, :],
            band_bias(jnp.logical_or(not_first_unit, t > 0)))
        cls, n = t // per_class4, t % per_class4
        cur, prv = block_rows(n)
        score_s[slot, 1] = _band_scores(
            q4_ref[cls, cur, :], jnp.where(n == 0, k4p_ref[cls], k4_ref[cls, prv, :]),
            k4_ref[cls, cur, :], band_bias(jnp.logical_or(not_first_unit, n > 0)))
        score_s[slot, 2] = _band_scores(q16_ref[t], k16p_ref[t], k16_ref[t],
                                        band_bias(not_first_unit))

    def put(pattern, rows, res):
        u, m, l = res
        out_s[pattern, rows, :] = u
        max_s[pattern, rows, :] = m
        den_s[pattern, rows, :] = l

    def softmax_pv(t, slot):
        cur, prv = block_rows(t)
        put(0, cur, _band_softmax_pv(
            score_s.at[slot, 0], prob_s.at[slot, 0],
            jnp.where(t == 0, v1p_ref[...], v1_ref[prv, :]), v1_ref[cur, :]))
        cls, n = t // per_class4, t % per_class4
        cur, prv = block_rows(n)
        put(1, pl.ds(cls + 4 * BLOCK * n, BLOCK, stride=4), _band_softmax_pv(
            score_s.at[slot, 1], prob_s.at[slot, 1],
            jnp.where(n == 0, v4p_ref[cls], v4_ref[cls, prv, :]), v4_ref[cls, cur, :]))
        put(2, pl.ds(t, BLOCK, stride=16), _band_softmax_pv(
            score_s.at[slot, 2], prob_s.at[slot, 2], v16p_ref[t], v16_ref[t]))

    scores(0, 0)

    def body(k, carry):
        t = 2 * k
        scores(t + 1, 1)
        softmax_pv(t, 0)
        scores(jnp.minimum(t + 2, blocks_per_unit - 1), 0)
        softmax_pv(t + 1, 1)
        return carry

    lax.fori_loop(0, blocks_per_unit // 2, body, 0)

    chunk = 256
    for c0 in range(0, UNIT, chunk):
        rows = slice(c0, c0 + chunk)
        m0, m1, m2 = max_s[0, rows, :], max_s[1, rows, :], max_s[2, rows, :]
        mx = jnp.maximum(jnp.maximum(m0, m1), m2)
        w0, w1, w2 = jnp.exp2(m0 - mx), jnp.exp2(m1 - mx), jnp.exp2(m2 - mx)
        num = w0 * out_s[0, rows, :] + w1 * out_s[1, rows, :] + w2 * out_s[2, rows, :]
        den = w0 * den_s[0, rows, :] + w1 * den_s[1, rows, :] + w2 * den_s[2, rows, :]
        o_ref[rows, :] = (num / den * gate_ref[rows, :].astype(jnp.float32)).astype(o_ref.dtype)


def _band_bias_table():
    i = np.arange(2 * BLOCK)[:, None] % BLOCK
    j = np.arange(2 * BLOCK)[None, :]
    band = (j >= i) & (j <= i + BLOCK)
    table = np.stack([band & (j >= BLOCK), band])
    return jnp.asarray(np.where(table, 0.0, NEG_BIG), dtype=jnp.float32)


def _dil_attn(q1, k1, v1, q4, k4, v4, q16, k16, v16, gate):
    s = q1.shape[1]
    n_units = s // UNIT
    blocks_per_unit = UNIT // BLOCK
    n4 = UNIT // 4
    prev_unit = lambda u: jnp.maximum(u - 1, 0)
    nat = pl.BlockSpec((None, UNIT, LANES), lambda u, p: (p, u, 0))
    nat_prev = pl.BlockSpec((None, BLOCK, LANES),
                            lambda u, p: (p, jnp.maximum(u * blocks_per_unit - 1, 0), 0))
    c4 = pl.BlockSpec((None, None, 4, n4, LANES), lambda u, p: (p, u, 0, 0, 0))
    c4_prev = pl.BlockSpec((None, None, 4, BLOCK, LANES),
                           lambda u, p: (p, prev_unit(u), 0, n4 // BLOCK - 1, 0))
    c16 = pl.BlockSpec((None, None, 16, BLOCK, LANES), lambda u, p: (p, u, 0, 0, 0))
    c16_prev = pl.BlockSpec((None, None, 16, BLOCK, LANES), lambda u, p: (p, prev_unit(u), 0, 0, 0))
    bias = _band_bias_table()
    return pl.pallas_call(
        _dil_attn_kernel,
        grid=(n_units, N_PAIR_DIL),
        in_specs=[nat, nat, nat, nat_prev, nat_prev,
                  c4, c4, c4, c4_prev, c4_prev,
                  c16, c16, c16, c16_prev, c16_prev,
                  pl.BlockSpec((UNIT, LANES), lambda u, p: (u, N_PAIR_SB + p)),
                  pl.BlockSpec(bias.shape, lambda u, p: (0, 0, 0))],
        out_specs=pl.BlockSpec((UNIT, LANES), lambda u, p: (u, p)),
        out_shape=jax.ShapeDtypeStruct((s, W_DIL), jnp.bfloat16),
        scratch_shapes=[pltpu.VMEM((3, UNIT, LANES), jnp.float32),
                        pltpu.VMEM((3, UNIT, LANES), jnp.float32),
                        pltpu.VMEM((3, UNIT, LANES), jnp.float32),
                        pltpu.VMEM((2, 3, 2 * BLOCK, 2 * BLOCK), jnp.float32),
                        pltpu.VMEM((2, 3, 2 * BLOCK, 2 * BLOCK), jnp.bfloat16)],
        compiler_params=pltpu.CompilerParams(
            dimension_semantics=("arbitrary", "arbitrary"), vmem_limit_bytes=VMEM_LIMIT_BYTES),
        name="dil_attn",
    )(q1, k1, v1, k1, v1, q4, k4, v4, k4, v4, q16, k16, v16, k16, v16, gate, bias)


def _out_proj_kernel(x_ref, ma_ref, md_ref, w_ref, o_ref):
    y = jnp.dot(ma_ref[...], w_ref[0:W_SB, :], preferred_element_type=jnp.float32)
    y = y + jnp.dot(md_ref[...], w_ref[W_SB:, :], preferred_element_type=jnp.float32)
    o_ref[...] = x_ref[...] + y


def _out_proj(layer, x, mix_a, mix_d, w_bf16):
    s, d = x.shape
    tm = ROW_TILE
    row = lambda i: (i, 0)
    return pl.pallas_call(
        _out_proj_kernel,
        grid=(s // tm,),
        in_specs=[pl.BlockSpec((tm, d), row), pl.BlockSpec((tm, W_SB), row),
                  pl.BlockSpec((tm, W_DIL), row),
                  pl.BlockSpec((None,) + w_bf16.shape[1:], lambda i: (layer, 0, 0))],
        out_specs=pl.BlockSpec((tm, d), row),
        out_shape=jax.ShapeDtypeStruct((s, d), x.dtype),
        compiler_params=pltpu.CompilerParams(
            dimension_semantics=("arbitrary",), vmem_limit_bytes=VMEM_LIMIT_BYTES),
        name="out_proj",
    )(x, mix_a, mix_d, w_bf16)


def _rope_tables(s):
    inv_freq = 1.0 / (ROPE_THETA ** (jnp.arange(ROPE_HALF, dtype=jnp.float32) * 2.0 / ROPE_DIM))
    ang = jnp.arange(s).astype(jnp.float32)[:, None] * inv_freq[None, :]
    cos, sin = jnp.cos(ang), jnp.sin(ang)
    rest = HEAD_DIM - ROPE_DIM
    zeros = lambda n: jnp.zeros((s, n), jnp.float32)
    cos_t = jnp.concatenate([cos, cos, jnp.ones((s, rest), jnp.float32)], axis=-1)
    sin_a = jnp.concatenate([-sin, zeros(HEAD_DIM - ROPE_HALF)], axis=-1)
    sin_b = jnp.concatenate([zeros(ROPE_HALF), sin, zeros(rest)], axis=-1)
    two = lambda t: jnp.tile(t, (1, LANES // HEAD_DIM))
    return two(cos_t), two(sin_a), two(sin_b)


def kernel(x, norm_g, w_in, q_norm_g, k_norm_g, w_out):
    b, s, d = x.shape
    assert s % UNIT == 0 and d == W_SB + W_DIL
    depth = w_in.shape[0]
    cos_t, sin_a, sin_b = _rope_tables(s)
    w_in_bf = w_in.astype(jnp.bfloat16)
    w_out_bf = w_out.astype(jnp.bfloat16)
    two = lambda g: jnp.tile(g.reshape(depth, 1, HEAD_DIM), (1, 1, LANES // HEAD_DIM))
    norm_g3, q_gain, k_gain = norm_g.reshape(depth, 1, d), two(q_norm_g), two(k_norm_g)
    outs = []
    for bi in range(b):
        xb = x[bi]
        prev = None
        for layer in range(depth):
            res = _in_proj(layer, xb, norm_g3, w_in_bf, q_gain, k_gain, cos_t, sin_a, sin_b, prev)
            if prev is not None:
                xb, res = res[0], res[1:]
            (qa, ka, va, gate, q1, k1, v1, q4, k4, v4, q16, k16, v16) = res
            mix_a = _sb_attn(qa, ka, va, gate)
            mix_d = _dil_attn(q1, k1, v1, q4, k4, v4, q16, k16, v16, gate)
            prev = (mix_a, mix_d, w_out_bf)
        outs.append(_out_proj(depth - 1, xb, mix_a, mix_d, w_out_bf))
    return outs[0][None] if b == 1 else jnp.stack(outs, axis=0)
```

```python
import math

import jax
import jax.numpy as jnp
import numpy as np
from jax import lax
from jax.experimental import pallas as pl
from jax.experimental.pallas import tpu as pltpu

HEAD_DIM = 64
N_HEADS_SB = 4
N_HEADS_DIL = 12
W_SB = N_HEADS_SB * HEAD_DIM
W_DIL = N_HEADS_DIL * HEAD_DIM
DIL_PATTERNS = ((128, 1), (512, 4), (2048, 16))
ROPE_THETA = 500000.0
ROPE_DIM = HEAD_DIM // 4
ROPE_HALF = ROPE_DIM // 2
BLOCK = 128
EPS = 1e-6
QK_SCALE = 1.0 / math.sqrt(HEAD_DIM)
LOG2_E = math.log2(math.e)

LANES = 128
UNIT = 16 * BLOCK
N_PAIR_DIL = W_DIL // LANES
N_PAIR_SB = W_SB // LANES
ROW_TILE = 512
SOFTMAX_ROWS = 64
SB_Q_BLOCKS = 2
VMEM_LIMIT_BYTES = 56 * 1024 * 1024

LOG_SURVIVAL_FLOOR = -90.0
NEG_BIG = -1e30


def _split_bf16(x):
    hi = x.astype(jnp.bfloat16)
    lo = (x - hi.astype(jnp.float32)).astype(jnp.bfloat16)
    return hi, lo


def _lane_iota(shape):
    return lax.broadcasted_iota(jnp.int32, shape, 1)


def _row_iota(shape):
    return lax.broadcasted_iota(jnp.int32, shape, 0)


def _silu(x):
    return x * (1.0 / (1.0 + jnp.exp(-x)))


def _stack_heads(t):
    lane = _lane_iota(t.shape)
    zero = jnp.zeros_like(t)
    return jnp.concatenate([jnp.where(lane < HEAD_DIM, t, zero),
                            jnp.where(lane >= HEAD_DIM, t, zero)], axis=0)


def _unstack_heads(t):
    n = t.shape[0] // 2
    top, bot = t[:n], t[n:]
    return jnp.where(_lane_iota(top.shape) < HEAD_DIM, top, bot)


def _out_proj_product(ma_ref, md_ref, wo_ref):
    mix_d = jnp.concatenate([md_ref[pair] for pair in range(N_PAIR_DIL)], axis=1)
    y = jnp.dot(ma_ref[...], wo_ref[0:W_SB, :], preferred_element_type=jnp.float32)
    return y + jnp.dot(mix_d, wo_ref[W_SB:, :], preferred_element_type=jnp.float32)


def _in_proj_kernel(x_ref, *refs):
    _in_proj_body(x_ref[...], *refs)


def _out_in_proj_kernel(x_ref, ma_ref, md_ref, wo_ref, *refs):
    (g_ref, w_ref, qg_ref, kg_ref, cos_ref, sa_ref, sb_ref, xn_ref, *out_and_scratch) = refs
    x = x_ref[...] + _out_proj_product(ma_ref, md_ref, wo_ref)
    xn_ref[...] = x
    _in_proj_body(x, g_ref, w_ref, qg_ref, kg_ref, cos_ref, sa_ref, sb_ref, *out_and_scratch)


def _in_proj_body(x, g_ref, w_ref, qg_ref, kg_ref, cos_ref, sa_ref, sb_ref,
                  qa_ref, ka_ref, va_ref, gate_a_ref, gate_d_ref,
                  q1_ref, k1_ref, v1_ref, q4_ref, k4_ref, v4_ref, q16_ref, k16_ref, v16_ref,
                  slab_ref, slab4_ref):
    tm = x.shape[0]
    ms = jnp.mean(x * x, axis=-1, keepdims=True)
    h = (x * lax.rsqrt(ms + EPS) * g_ref[...]).astype(jnp.bfloat16)

    def proj(c0, width):
        return jnp.dot(h, w_ref[:, c0:c0 + width], preferred_element_type=jnp.float32)

    pa = proj(0, 4 * W_SB)
    qa_ref[...] = (pa[:, 0:W_SB] * QK_SCALE).astype(jnp.bfloat16)
    ka_ref[...] = pa[:, W_SB:2 * W_SB].astype(jnp.bfloat16)
    va_ref[...] = pa[:, 2 * W_SB:3 * W_SB].astype(jnp.bfloat16)
    gate_a_ref[...] = _silu(pa[:, 3 * W_SB:]).astype(jnp.bfloat16)
    base = 4 * W_SB
    gate_d = _silu(proj(base + 3 * W_DIL, W_DIL)).astype(jnp.bfloat16)
    for pair in range(N_PAIR_DIL):
        gate_d_ref[pair] = gate_d[:, pair * LANES:(pair + 1) * LANES]

    r = (_row_iota((2 * LANES, LANES)) % LANES) // HEAD_DIM
    c = _lane_iota((2 * LANES, LANES)) // HEAD_DIM
    head_mean = jnp.where(r == c, 1.0 / HEAD_DIM, 0.0).astype(jnp.bfloat16)

    cos = cos_ref[...]
    sin_a = sa_ref[...]
    sin_b = sb_ref[...]

    def emit(pair, t, o1_ref, o4_ref, o16_ref):
        o1_ref[pair] = t.astype(jnp.bfloat16)
        slab_ref[pair] = t
        n4 = tm // 4
        for c4 in range(4):
            t4 = slab_ref[pair, pl.ds(c4, n4, stride=4), :]
            o4_ref[pair, c4] = t4.astype(jnp.bfloat16)
            slab4_ref[pair, c4 * n4:(c4 + 1) * n4, :] = t4
        n16 = tm // 16
        for c4 in range(4):
            for b in range(4):
                t16 = slab4_ref[pair, pl.ds(c4 * n4 + b, n16, stride=4), :]
                o16_ref[pair, c4 + 4 * b] = t16.astype(jnp.bfloat16)

    def qk_norm_rope(t, gain, scale):
        hi, lo = _split_bf16(t * t)
        msq = jnp.dot(jnp.concatenate([hi, lo], axis=1), head_mean,
                      preferred_element_type=jnp.float32)
        t = t * lax.rsqrt(msq + EPS) * gain
        up = pltpu.roll(t, LANES - ROPE_HALF, axis=1)
        down = pltpu.roll(t, ROPE_HALF, axis=1)
        t = t * cos + up * sin_a + down * sin_b
        if scale != 1.0:
            t = t * scale
        return t

    pq = proj(base, W_DIL)
    for pair in range(N_PAIR_DIL):
        t = qk_norm_rope(pq[:, pair * LANES:(pair + 1) * LANES], qg_ref[...], QK_SCALE * LOG2_E)
        emit(pair, t, q1_ref, q4_ref, q16_ref)
    pk = proj(base + W_DIL, W_DIL)
    for pair in range(N_PAIR_DIL):
        t = qk_norm_rope(pk[:, pair * LANES:(pair + 1) * LANES], kg_ref[...], 1.0)
        emit(pair, t, k1_ref, k4_ref, k16_ref)
    pv = proj(base + 2 * W_DIL, W_DIL)
    for pair in range(N_PAIR_DIL):
        emit(pair, pv[:, pair * LANES:(pair + 1) * LANES], v1_ref, v4_ref, v16_ref)


def _in_proj(layer, x, g, w_bf16, qg, kg, cos_t, sin_a, sin_b, prev=None):
    s, d = x.shape
    tm = ROW_TILE
    per_unit = UNIT // tm
    n_units = s // UNIT
    bf = jnp.bfloat16
    row = lambda i: (i, 0)
    nat_spec = pl.BlockSpec((N_PAIR_DIL, tm, LANES), lambda i: (0, i, 0))
    c4_spec = pl.BlockSpec((N_PAIR_DIL, None, 4, tm // 4, LANES),
                           lambda i: (0, i // per_unit, 0, i % per_unit, 0))
    c16_spec = pl.BlockSpec((N_PAIR_DIL, None, 16, tm // 16, LANES),
                            lambda i: (0, i // per_unit, 0, i % per_unit, 0))
    nat_shape = jax.ShapeDtypeStruct((N_PAIR_DIL, s, LANES), bf)
    c4_shape = jax.ShapeDtypeStruct((N_PAIR_DIL, n_units, 4, UNIT // 4, LANES), bf)
    c16_shape = jax.ShapeDtypeStruct((N_PAIR_DIL, n_units, 16, UNIT // 16, LANES), bf)
    sb_shape = jax.ShapeDtypeStruct((s, W_SB), bf)
    in_specs = [
        pl.BlockSpec((None, 1, d), lambda i: (layer, 0, 0)),
        pl.BlockSpec((None,) + w_bf16.shape[1:], lambda i: (layer, 0, 0)),
        pl.BlockSpec((None, 1, LANES), lambda i: (layer, 0, 0)),
        pl.BlockSpec((None, 1, LANES), lambda i: (layer, 0, 0)),
        pl.BlockSpec((tm, LANES), row),
        pl.BlockSpec((tm, LANES), row),
        pl.BlockSpec((tm, LANES), row),
    ]
    operands = [g, w_bf16, qg, kg, cos_t, sin_a, sin_b]
    out_specs = [
        pl.BlockSpec((tm, W_SB), row), pl.BlockSpec((tm, W_SB), row),
        pl.BlockSpec((tm, W_SB), row), pl.BlockSpec((tm, W_SB), row), nat_spec,
        nat_spec, nat_spec, nat_spec, c4_spec, c4_spec, c4_spec,
        c16_spec, c16_spec, c16_spec,
    ]
    out_shape = [sb_shape, sb_shape, sb_shape, sb_shape, nat_shape,
                 nat_shape, nat_shape, nat_shape, c4_shape, c4_shape, c4_shape,
                 c16_shape, c16_shape, c16_shape]
    x_spec = pl.BlockSpec((tm, d), row)
    if prev is None:
        body, name = _in_proj_kernel, "in_proj"
        in_specs = [x_spec] + in_specs
        operands = [x] + operands
    else:
        mix_a, mix_d, w_out = prev
        body, name = _out_in_proj_kernel, "out_in_proj"
        in_specs = [x_spec, pl.BlockSpec((tm, W_SB), row), nat_spec,
                    pl.BlockSpec((None,) + w_out.shape[1:], lambda i: (layer - 1, 0, 0))] + in_specs
        operands = [x, mix_a, mix_d, w_out] + operands
        out_specs = [x_spec] + out_specs
        out_shape = [jax.ShapeDtypeStruct((s, d), x.dtype)] + out_shape
    return pl.pallas_call(
        body,
        grid=(s // tm,),
        in_specs=in_specs,
        out_specs=out_specs,
        out_shape=out_shape,
        scratch_shapes=[pltpu.VMEM((N_PAIR_DIL, tm, LANES), jnp.float32),
                        pltpu.VMEM((N_PAIR_DIL, tm, LANES), jnp.float32)],
        compiler_params=pltpu.CompilerParams(
            dimension_semantics=("arbitrary",), vmem_limit_bytes=VMEM_LIMIT_BYTES),
        name=name,
    )(*operands)


def _spread(per_head):
    return jnp.concatenate([jnp.broadcast_to(t, (BLOCK, BLOCK)) for t in per_head], axis=1)


def _sb_tile(q, kb, vb, later, mask, tri2):
    z = lax.dot_general(q, _stack_heads(kb), (((1,), (1,)), ((), ())),
                        preferred_element_type=jnp.float32)
    soft = jnp.log(1.0 + jnp.exp(-jnp.abs(z)))
    log_surv = -(jnp.maximum(z, 0.0) + soft)
    log_beta = jnp.minimum(z, 0.0) - soft
    if mask is not None:
        log_surv = jnp.where(mask, log_surv, 0.0)
    hi, lo = _split_bf16(log_surv)
    w = jnp.dot(jnp.concatenate([hi, lo], axis=0), tri2, preferred_element_type=jnp.float32)
    within = w[:BLOCK] + w[BLOCK:]
    a = jnp.exp(log_beta + within + later)
    if mask is not None:
        a = jnp.where(mask, a, 0.0)
    pv = jnp.dot(a.astype(jnp.bfloat16), _stack_heads(vb), preferred_element_type=jnp.float32)
    tot = [jnp.sum(log_surv[:, hd * BLOCK:(hd + 1) * BLOCK], axis=-1, keepdims=True)
           for hd in range(2)]
    return pv, tot


def _sb_attn_kernel(q_ref, k_ref, v_ref, gate_ref, tri2_ref, o_ref, acc_ref, later_ref):
    step = pl.program_id(0)
    tri2 = tri2_ref[...]
    shape2 = (BLOCK, 2 * BLOCK)
    strictly_before = _lane_iota(shape2) % BLOCK < _row_iota(shape2)

    def key_rows(j):
        return pl.ds(pl.multiple_of(j * BLOCK, BLOCK), BLOCK)

    qs, more = [], []
    for sub in range(SB_Q_BLOCKS):
        i = step * SB_Q_BLOCKS + sub
        rows = slice(sub * BLOCK, (sub + 1) * BLOCK)
        diag, prev = key_rows(i), key_rows(jnp.maximum(i - 1, 0))
        no_prev = jnp.where(i > 0, 0.0, NEG_BIG)
        worst = None
        for pair in range(N_PAIR_SB):
            lanes = slice(pair * LANES, (pair + 1) * LANES)
            slot = sub * N_PAIR_SB + pair
            q = q_ref[rows, lanes]
            pv0, tot0 = _sb_tile(q, k_ref[diag, lanes], v_ref[diag, lanes], 0.0,
                                 strictly_before, tri2)
            pv1, tot1 = _sb_tile(q, k_ref[prev, lanes], v_ref[prev, lanes],
                                 _spread(tot0) + no_prev, None, tri2)
            later = _spread([tot0[0] + tot1[0], tot0[1] + tot1[1]])
            acc_ref[slot] = pv0 + pv1
            later_ref[slot] = later
            qs.append(q)
            worst = later if worst is None else jnp.maximum(worst, later)
        more.append(jnp.max(worst) > LOG_SURVIVAL_FLOOR)

    for sub in range(SB_Q_BLOCKS):
        def cond(carry):
            j, live = carry
            return jnp.logical_and(j >= 0, live)

        def body(carry, sub=sub):
            j, _ = carry
            rows = key_rows(j)
            worst = None
            for pair in range(N_PAIR_SB):
                lanes = slice(pair * LANES, (pair + 1) * LANES)
                slot = sub * N_PAIR_SB + pair
                later = later_ref[slot]
                pv, tot = _sb_tile(qs[slot], k_ref[rows, lanes], v_ref[rows, lanes], later, None, tri2)
                acc_ref[slot] += pv
                later = later + _spread(tot)
                later_ref[slot] = later
                worst = later if worst is None else jnp.maximum(worst, later)
            return j - 1, jnp.max(worst) > LOG_SURVIVAL_FLOOR

        lax.while_loop(cond, body, (step * SB_Q_BLOCKS + sub - 2, more[sub]))

    for sub in range(SB_Q_BLOCKS):
        rows = slice(sub * BLOCK, (sub + 1) * BLOCK)
        for pair in range(N_PAIR_SB):
            lanes = slice(pair * LANES, (pair + 1) * LANES)
            gate = gate_ref[rows, lanes].astype(jnp.float32)
            o_ref[rows, lanes] = (acc_ref[sub * N_PAIR_SB + pair] * gate).astype(o_ref.dtype)


def _sb_attn(qa, ka, va, gate):
    s = qa.shape[0]
    tq = SB_Q_BLOCKS * BLOCK
    blk = lambda i: (i, 0)
    resident = pl.BlockSpec((s, W_SB), lambda i: (0, 0), pipeline_mode=pl.Buffered(1))
    idx = np.arange(2 * BLOCK)
    tri2 = jnp.asarray((idx[:, None] > idx[None, :]) & (idx[:, None] // BLOCK == idx[None, :] // BLOCK),
                       dtype=jnp.bfloat16)
    n_slots = SB_Q_BLOCKS * N_PAIR_SB
    return pl.pallas_call(
        _sb_attn_kernel,
        grid=(s // tq,),
        in_specs=[pl.BlockSpec((tq, W_SB), blk), resident, resident,
                  pl.BlockSpec((tq, W_SB), blk),
                  pl.BlockSpec(tri2.shape, lambda i: (0, 0))],
        out_specs=pl.BlockSpec((tq, W_SB), blk),
        out_shape=jax.ShapeDtypeStruct((s, W_SB), jnp.bfloat16),
        scratch_shapes=[pltpu.VMEM((n_slots, BLOCK, LANES), jnp.float32),
                        pltpu.VMEM((n_slots, BLOCK, 2 * BLOCK), jnp.float32)],
        compiler_params=pltpu.CompilerParams(
            dimension_semantics=("arbitrary",), vmem_limit_bytes=VMEM_LIMIT_BYTES),
        name="sb_attn",
    )(qa, ka, va, gate, tri2)


def _band_scores(q, k_prev, k_own, bias):
    kk = jnp.concatenate([k_prev, k_own], axis=0)
    return lax.dot_general(_stack_heads(q), kk, (((1,), (1,)), ((), ())),
                           preferred_element_type=jnp.float32) + bias


def _band_softmax_pv(s_ref, p_ref, v_prev, v_own):
    vv = jnp.concatenate([v_prev, v_own], axis=0)
    ms, ls = [], []
    for r0 in range(0, 2 * BLOCK, SOFTMAX_ROWS):
        rows = slice(r0, r0 + SOFTMAX_ROWS)
        s = s_ref[rows, :]
        m = jnp.max(s, axis=-1, keepdims=True)
        p = jnp.exp2(s - m)
        ms.append(m)
        ls.append(jnp.sum(p, axis=-1, keepdims=True))
        p_ref[rows, :] = p.astype(jnp.bfloat16)
    u = jnp.dot(p_ref[...], vv, preferred_element_type=jnp.float32)

    def unstack_columns(cols):
        half = len(cols) // 2
        head0 = _lane_iota((SOFTMAX_ROWS, LANES)) < HEAD_DIM
        return jnp.concatenate(
            [jnp.where(head0, jnp.broadcast_to(cols[c], (SOFTMAX_ROWS, LANES)),
                       jnp.broadcast_to(cols[half + c], (SOFTMAX_ROWS, LANES)))
             for c in range(half)], axis=0)

    return _unstack_heads(u), unstack_columns(ms), unstack_columns(ls)


def _dil_attn_kernel(q1_ref, k1_ref, v1_ref, k1p_ref, v1p_ref,
                     q4_ref, k4_ref, v4_ref, k4p_ref, v4p_ref,
                     q16_ref, k16_ref, v16_ref, k16p_ref, v16p_ref,
                     gate_ref, bias_ref, o_ref, out_s, max_s, den_s, score_s, prob_s):
    not_first_unit = pl.program_id(0) > 0
    blocks_per_unit = UNIT // BLOCK
    per_class4 = blocks_per_unit // 4

    def block_rows(n):
        cur = pl.ds(pl.multiple_of(n * BLOCK, BLOCK), BLOCK)
        prv = pl.ds(pl.multiple_of(jnp.maximum(n - 1, 0) * BLOCK, BLOCK), BLOCK)
        return cur, prv

    def band_bias(has_prev):
        return bias_ref[has_prev.astype(jnp.int32)]

    def scores(t, slot):
        cur, prv = block_rows(t)
        score_s[slot, 0] = _band_scores(
            q1_ref[cur, :], jnp.where(t == 0, k1p_ref[...], k1_ref[prv, :]), k1_ref[cur, :],
            band_bias(jnp.logical_or(not_first_unit, t > 0)))
        cls, n = t // per_class4, t % per_class4
        cur, prv = block_rows(n)
        score_s[slot, 1] = _band_scores(
            q4_ref[cls, cur, :], jnp.where(n == 0, k4p_ref[cls], k4_ref[cls, prv, :]),
            k4_ref[cls, cur, :], band_bias(jnp.logical_or(not_first_unit, n > 0)))
        score_s[slot, 2] = _band_scores(q16_ref[t], k16p_ref[t], k16_ref[t],
                                        band_bias(not_first_unit))

    def put(pattern, rows, res):
        u, m, l = res
        out_s[pattern, rows, :] = u
        max_s[pattern, rows, :] = m
        den_s[pattern, rows, :] = l

    def softmax_pv(t, slot):
        cur, prv = block_rows(t)
        put(0, cur, _band_softmax_pv(
            score_s.at[slot, 0], prob_s.at[slot, 0],
            jnp.where(t == 0, v1p_ref[...], v1_ref[prv, :]), v1_ref[cur, :]))
        cls, n = t // per_class4, t % per_class4
        cur, prv = block_rows(n)
        put(1, pl.ds(cls + 4 * BLOCK * n, BLOCK, stride=4), _band_softmax_pv(
            score_s.at[slot, 1], prob_s.at[slot, 1],
            jnp.where(n == 0, v4p_ref[cls], v4_ref[cls, prv, :]), v4_ref[cls, cur, :]))
        put(2, pl.ds(t, BLOCK, stride=16), _band_softmax_pv(
            score_s.at[slot, 2], prob_s.at[slot, 2], v16p_ref[t], v16_ref[t]))

    scores(0, 0)

    def body(k, carry):
        t = 2 * k
        scores(t + 1, 1)
        softmax_pv(t, 0)
        scores(jnp.minimum(t + 2, blocks_per_unit - 1), 0)
        softmax_pv(t + 1, 1)
        return carry

    lax.fori_loop(0, blocks_per_unit // 2, body, 0)

    chunk = 256
    for c0 in range(0, UNIT, chunk):
        rows = slice(c0, c0 + chunk)
        m0, m1, m2 = max_s[0, rows, :], max_s[1, rows, :], max_s[2, rows, :]
        mx = jnp.maximum(jnp.maximum(m0, m1), m2)
        w0, w1, w2 = jnp.exp2(m0 - mx), jnp.exp2(m1 - mx), jnp.exp2(m2 - mx)
        num = w0 * out_s[0, rows, :] + w1 * out_s[1, rows, :] + w2 * out_s[2, rows, :]
        den = w0 * den_s[0, rows, :] + w1 * den_s[1, rows, :] + w2 * den_s[2, rows, :]
        o_ref[rows, :] = (num / den * gate_ref[rows, :].astype(jnp.float32)).astype(o_ref.dtype)


def _band_bias_table():
    i = np.arange(2 * BLOCK)[:, None] % BLOCK
    j = np.arange(2 * BLOCK)[None, :]
    band = (j >= i) & (j <= i + BLOCK)
    table = np.stack([band & (j >= BLOCK), band])
    return jnp.asarray(np.where(table, 0.0, NEG_BIG), dtype=jnp.float32)


def _dil_attn(q1, k1, v1, q4, k4, v4, q16, k16, v16, gate):
    s = q1.shape[1]
    n_units = s // UNIT
    blocks_per_unit = UNIT // BLOCK
    n4 = UNIT // 4
    prev_unit = lambda u: jnp.maximum(u - 1, 0)
    nat = pl.BlockSpec((None, UNIT, LANES), lambda u, p: (p, u, 0))
    nat_prev = pl.BlockSpec((None, BLOCK, LANES),
                            lambda u, p: (p, jnp.maximum(u * blocks_per_unit - 1, 0), 0))
    c4 = pl.BlockSpec((None, None, 4, n4, LANES), lambda u, p: (p, u, 0, 0, 0))
    c4_prev = pl.BlockSpec((None, None, 4, BLOCK, LANES),
                           lambda u, p: (p, prev_unit(u), 0, n4 // BLOCK - 1, 0))
    c16 = pl.BlockSpec((None, None, 16, BLOCK, LANES), lambda u, p: (p, u, 0, 0, 0))
    c16_prev = pl.BlockSpec((None, None, 16, BLOCK, LANES), lambda u, p: (p, prev_unit(u), 0, 0, 0))
    bias = _band_bias_table()
    return pl.pallas_call(
        _dil_attn_kernel,
        grid=(n_units, N_PAIR_DIL),
        in_specs=[nat, nat, nat, nat_prev, nat_prev,
                  c4, c4, c4, c4_prev, c4_prev,
                  c16, c16, c16, c16_prev, c16_prev,
                  nat, pl.BlockSpec(bias.shape, lambda u, p: (0, 0, 0))],
        out_specs=nat,
        out_shape=jax.ShapeDtypeStruct((N_PAIR_DIL, s, LANES), jnp.bfloat16),
        scratch_shapes=[pltpu.VMEM((3, UNIT, LANES), jnp.float32),
                        pltpu.VMEM((3, UNIT, LANES), jnp.float32),
                        pltpu.VMEM((3, UNIT, LANES), jnp.float32),
                        pltpu.VMEM((2, 3, 2 * BLOCK, 2 * BLOCK), jnp.float32),
                        pltpu.VMEM((2, 3, 2 * BLOCK, 2 * BLOCK), jnp.bfloat16)],
        compiler_params=pltpu.CompilerParams(
            dimension_semantics=("arbitrary", "arbitrary"), vmem_limit_bytes=VMEM_LIMIT_BYTES),
        name="dil_attn",
    )(q1, k1, v1, k1, v1, q4, k4, v4, k4, v4, q16, k16, v16, k16, v16, gate, bias)


def _out_proj_kernel(x_ref, ma_ref, md_ref, w_ref, o_ref):
    o_ref[...] = x_ref[...] + _out_proj_product(ma_ref, md_ref, w_ref)


def _out_proj(layer, x, mix_a, mix_d, w_bf16):
    s, d = x.shape
    tm = ROW_TILE
    row = lambda i: (i, 0)
    return pl.pallas_call(
        _out_proj_kernel,
        grid=(s // tm,),
        in_specs=[pl.BlockSpec((tm, d), row), pl.BlockSpec((tm, W_SB), row),
                  pl.BlockSpec((N_PAIR_DIL, tm, LANES), lambda i: (0, i, 0)),
                  pl.BlockSpec((None,) + w_bf16.shape[1:], lambda i: (layer, 0, 0))],
        out_specs=pl.BlockSpec((tm, d), row),
        out_shape=jax.ShapeDtypeStruct((s, d), x.dtype),
        compiler_params=pltpu.CompilerParams(
            dimension_semantics=("arbitrary",), vmem_limit_bytes=VMEM_LIMIT_BYTES),
        name="out_proj",
    )(x, mix_a, mix_d, w_bf16)


def _rope_tables(s):
    inv_freq = 1.0 / (ROPE_THETA ** (jnp.arange(ROPE_HALF, dtype=jnp.float32) * 2.0 / ROPE_DIM))
    ang = jnp.arange(s).astype(jnp.float32)[:, None] * inv_freq[None, :]
    cos, sin = jnp.cos(ang), jnp.sin(ang)
    rest = HEAD_DIM - ROPE_DIM
    zeros = lambda n: jnp.zeros((s, n), jnp.float32)
    cos_t = jnp.concatenate([cos, cos, jnp.ones((s, rest), jnp.float32)], axis=-1)
    sin_a = jnp.concatenate([-sin, zeros(HEAD_DIM - ROPE_HALF)], axis=-1)
    sin_b = jnp.concatenate([zeros(ROPE_HALF), sin, zeros(rest)], axis=-1)
    two = lambda t: jnp.tile(t, (1, LANES // HEAD_DIM))
    return two(cos_t), two(sin_a), two(sin_b)


def kernel(x, norm_g, w_in, q_norm_g, k_norm_g, w_out):
    b, s, d = x.shape
    assert s % UNIT == 0 and d == W_SB + W_DIL
    depth = w_in.shape[0]
    cos_t, sin_a, sin_b = _rope_tables(s)
    w_in_bf = w_in.astype(jnp.bfloat16)
    w_out_bf = w_out.astype(jnp.bfloat16)
    two = lambda g: jnp.tile(g.reshape(depth, 1, HEAD_DIM), (1, 1, LANES // HEAD_DIM))
    norm_g3, q_gain, k_gain = norm_g.reshape(depth, 1, d), two(q_norm_g), two(k_norm_g)
    outs = []
    for bi in range(b):
        xb = x[bi]
        prev = None
        for layer in range(depth):
            res = _in_proj(layer, xb, norm_g3, w_in_bf, q_gain, k_gain, cos_t, sin_a, sin_b, prev)
            if prev is not None:
                xb, res = res[0], res[1:]
            (qa, ka, va, gate_a, gate_d, q1, k1, v1, q4, k4, v4, q16, k16, v16) = res
            mix_a = _sb_attn(qa, ka, va, gate_a)
            mix_d = _dil_attn(q1, k1, v1, q4, k4, v4, q16, k16, v16, gate_d)
            prev = (mix_a, mix_d, w_out_bf)
        outs.append(_out_proj(depth - 1, xb, mix_a, mix_d, w_out_bf))
    return outs[0][None] if b == 1 else jnp.stack(outs, axis=0)
```

```python
import math

import jax
import jax.numpy as jnp
import numpy as np
from jax import lax
from jax.experimental import pallas as pl
from jax.experimental.pallas import tpu as pltpu

HEAD_DIM = 64
N_HEADS_SB = 4
N_HEADS_DIL = 12
W_SB = N_HEADS_SB * HEAD_DIM
W_DIL = N_HEADS_DIL * HEAD_DIM
DIL_PATTERNS = ((128, 1), (512, 4), (2048, 16))
ROPE_THETA = 500000.0
ROPE_DIM = HEAD_DIM // 4
ROPE_HALF = ROPE_DIM // 2
BLOCK = 128
EPS = 1e-6
QK_SCALE = 1.0 / math.sqrt(HEAD_DIM)
LOG2_E = math.log2(math.e)

LANES = 128
UNIT = 16 * BLOCK
N_PAIR_DIL = W_DIL // LANES
N_PAIR_SB = W_SB // LANES
ROW_TILE = 512
SOFTMAX_ROWS = 64
SB_Q_BLOCKS = 4
VMEM_LIMIT_BYTES = 56 * 1024 * 1024

LOG_SURVIVAL_FLOOR = -90.0
NEG_BIG = -1e30


def _split_bf16(x):
    hi = x.astype(jnp.bfloat16)
    lo = (x - hi.astype(jnp.float32)).astype(jnp.bfloat16)
    return hi, lo


def _lane_iota(shape):
    return lax.broadcasted_iota(jnp.int32, shape, 1)


def _row_iota(shape):
    return lax.broadcasted_iota(jnp.int32, shape, 0)


def _silu(x):
    return x * (1.0 / (1.0 + jnp.exp(-x)))


def _stack_heads(t):
    lane = _lane_iota(t.shape)
    zero = jnp.zeros_like(t)
    return jnp.concatenate([jnp.where(lane < HEAD_DIM, t, zero),
                            jnp.where(lane >= HEAD_DIM, t, zero)], axis=0)


def _unstack_heads(t):
    n = t.shape[0] // 2
    top, bot = t[:n], t[n:]
    return jnp.where(_lane_iota(top.shape) < HEAD_DIM, top, bot)


def _out_proj_product(ma_ref, md_ref, wo_ref):
    mix_d = jnp.concatenate([md_ref[pair] for pair in range(N_PAIR_DIL)], axis=1)
    y = jnp.dot(ma_ref[...], wo_ref[0:W_SB, :], preferred_element_type=jnp.float32)
    return y + jnp.dot(mix_d, wo_ref[W_SB:, :], preferred_element_type=jnp.float32)


def _in_proj_kernel(x_ref, *refs):
    _in_proj_body(x_ref[...], *refs)


def _out_in_proj_kernel(x_ref, ma_ref, md_ref, wo_ref, *refs):
    (g_ref, w_ref, qg_ref, kg_ref, cos_ref, sa_ref, sb_ref, xn_ref, *out_and_scratch) = refs
    x = x_ref[...] + _out_proj_product(ma_ref, md_ref, wo_ref)
    xn_ref[...] = x
    _in_proj_body(x, g_ref, w_ref, qg_ref, kg_ref, cos_ref, sa_ref, sb_ref, *out_and_scratch)


def _in_proj_body(x, g_ref, w_ref, qg_ref, kg_ref, cos_ref, sa_ref, sb_ref,
                  qa_ref, ka_ref, va_ref, gate_a_ref, gate_d_ref,
                  q1_ref, k1_ref, v1_ref, q4_ref, k4_ref, v4_ref, q16_ref, k16_ref, v16_ref,
                  slab_ref, slab4_ref):
    tm = x.shape[0]
    ms = jnp.mean(x * x, axis=-1, keepdims=True)
    h = (x * lax.rsqrt(ms + EPS) * g_ref[...]).astype(jnp.bfloat16)

    def proj(c0, width):
        return jnp.dot(h, w_ref[:, c0:c0 + width], preferred_element_type=jnp.float32)

    base = 4 * W_SB

    r = (_row_iota((2 * LANES, LANES)) % LANES) // HEAD_DIM
    c = _lane_iota((2 * LANES, LANES)) // HEAD_DIM
    head_mean = jnp.where(r == c, 1.0 / HEAD_DIM, 0.0).astype(jnp.bfloat16)

    cos = cos_ref[...]
    sin_a = sa_ref[...]
    sin_b = sb_ref[...]

    def emit(pair, t, o1_ref, o4_ref, o16_ref):
        o1_ref[pair] = t.astype(jnp.bfloat16)
        slab_ref[pair] = t
        n4 = tm // 4
        for c4 in range(4):
            t4 = slab_ref[pair, pl.ds(c4, n4, stride=4), :]
            o4_ref[pair, c4] = t4.astype(jnp.bfloat16)
            slab4_ref[pair, c4 * n4:(c4 + 1) * n4, :] = t4
        n16 = tm // 16
        for c4 in range(4):
            for b in range(4):
                t16 = slab4_ref[pair, pl.ds(c4 * n4 + b, n16, stride=4), :]
                o16_ref[pair, c4 + 4 * b] = t16.astype(jnp.bfloat16)

    def qk_norm_rope(t, gain, scale):
        hi, lo = _split_bf16(t * t)
        msq = jnp.dot(jnp.concatenate([hi, lo], axis=1), head_mean,
                      preferred_element_type=jnp.float32)
        t = t * lax.rsqrt(msq + EPS) * gain
        up = pltpu.roll(t, LANES - ROPE_HALF, axis=1)
        down = pltpu.roll(t, ROPE_HALF, axis=1)
        t = t * cos + up * sin_a + down * sin_b
        if scale != 1.0:
            t = t * scale
        return t

    pq = proj(base, W_DIL)
    for pair in range(N_PAIR_DIL):
        t = qk_norm_rope(pq[:, pair * LANES:(pair + 1) * LANES], qg_ref[...], QK_SCALE * LOG2_E)
        emit(pair, t, q1_ref, q4_ref, q16_ref)
    pk = proj(base + W_DIL, W_DIL)
    for pair in range(N_PAIR_DIL):
        t = qk_norm_rope(pk[:, pair * LANES:(pair + 1) * LANES], kg_ref[...], 1.0)
        emit(pair, t, k1_ref, k4_ref, k16_ref)
    pv = proj(base + 2 * W_DIL, W_DIL)
    for pair in range(N_PAIR_DIL):
        emit(pair, pv[:, pair * LANES:(pair + 1) * LANES], v1_ref, v4_ref, v16_ref)

    pa = proj(0, 4 * W_SB)
    qa_ref[...] = (pa[:, 0:W_SB] * QK_SCALE).astype(jnp.bfloat16)
    ka_ref[...] = pa[:, W_SB:2 * W_SB].astype(jnp.bfloat16)
    va_ref[...] = pa[:, 2 * W_SB:3 * W_SB].astype(jnp.bfloat16)
    gate_a_ref[...] = _silu(pa[:, 3 * W_SB:]).astype(jnp.bfloat16)
    gate_d = _silu(proj(base + 3 * W_DIL, W_DIL)).astype(jnp.bfloat16)
    for pair in range(N_PAIR_DIL):
        gate_d_ref[pair] = gate_d[:, pair * LANES:(pair + 1) * LANES]


def _in_proj(layer, x, g, w_bf16, qg, kg, cos_t, sin_a, sin_b, prev=None):
    s, d = x.shape
    tm = ROW_TILE
    per_unit = UNIT // tm
    n_units = s // UNIT
    bf = jnp.bfloat16
    row = lambda i: (i, 0)
    nat_spec = pl.BlockSpec((N_PAIR_DIL, tm, LANES), lambda i: (0, i, 0))
    c4_spec = pl.BlockSpec((N_PAIR_DIL, None, 4, tm // 4, LANES),
                           lambda i: (0, i // per_unit, 0, i % per_unit, 0))
    c16_spec = pl.BlockSpec((N_PAIR_DIL, None, 16, tm // 16, LANES),
                            lambda i: (0, i // per_unit, 0, i % per_unit, 0))
    nat_shape = jax.ShapeDtypeStruct((N_PAIR_DIL, s, LANES), bf)
    c4_shape = jax.ShapeDtypeStruct((N_PAIR_DIL, n_units, 4, UNIT // 4, LANES), bf)
    c16_shape = jax.ShapeDtypeStruct((N_PAIR_DIL, n_units, 16, UNIT // 16, LANES), bf)
    sb_shape = jax.ShapeDtypeStruct((s, W_SB), bf)
    in_specs = [
        pl.BlockSpec((None, 1, d), lambda i: (layer, 0, 0)),
        pl.BlockSpec((None,) + w_bf16.shape[1:], lambda i: (layer, 0, 0)),
        pl.BlockSpec((None, 1, LANES), lambda i: (layer, 0, 0)),
        pl.BlockSpec((None, 1, LANES), lambda i: (layer, 0, 0)),
        pl.BlockSpec((tm, LANES), row),
        pl.BlockSpec((tm, LANES), row),
        pl.BlockSpec((tm, LANES), row),
    ]
    operands = [g, w_bf16, qg, kg, cos_t, sin_a, sin_b]
    out_specs = [
        pl.BlockSpec((tm, W_SB), row), pl.BlockSpec((tm, W_SB), row),
        pl.BlockSpec((tm, W_SB), row), pl.BlockSpec((tm, W_SB), row), nat_spec,
        nat_spec, nat_spec, nat_spec, c4_spec, c4_spec, c4_spec,
        c16_spec, c16_spec, c16_spec,
    ]
    out_shape = [sb_shape, sb_shape, sb_shape, sb_shape, nat_shape,
                 nat_shape, nat_shape, nat_shape, c4_shape, c4_shape, c4_shape,
                 c16_shape, c16_shape, c16_shape]
    x_spec = pl.BlockSpec((tm, d), row)
    if prev is None:
        body, name = _in_proj_kernel, "in_proj"
        in_specs = [x_spec] + in_specs
        operands = [x] + operands
    else:
        mix_a, mix_d, w_out = prev
        body, name = _out_in_proj_kernel, "out_in_proj"
        in_specs = [x_spec, pl.BlockSpec((tm, W_SB), row), nat_spec,
                    pl.BlockSpec((None,) + w_out.shape[1:], lambda i: (layer - 1, 0, 0))] + in_specs
        operands = [x, mix_a, mix_d, w_out] + operands
        out_specs = [x_spec] + out_specs
        out_shape = [jax.ShapeDtypeStruct((s, d), x.dtype)] + out_shape
    return pl.pallas_call(
        body,
        grid=(s // tm,),
        in_specs=in_specs,
        out_specs=out_specs,
        out_shape=out_shape,
        scratch_shapes=[pltpu.VMEM((N_PAIR_DIL, tm, LANES), jnp.float32),
                        pltpu.VMEM((N_PAIR_DIL, tm, LANES), jnp.float32)],
        compiler_params=pltpu.CompilerParams(
            dimension_semantics=("arbitrary",), vmem_limit_bytes=VMEM_LIMIT_BYTES),
        name=name,
    )(*operands)


def _spread(per_head):
    return jnp.concatenate([jnp.broadcast_to(t, (BLOCK, BLOCK)) for t in per_head], axis=1)


def _sb_tile(q, kb, vb, later, mask, tri2):
    z = lax.dot_general(q, _stack_heads(kb), (((1,), (1,)), ((), ())),
                        preferred_element_type=jnp.float32)
    soft = jnp.log(1.0 + jnp.exp(-jnp.abs(z)))
    log_surv = -(jnp.maximum(z, 0.0) + soft)
    log_beta = jnp.minimum(z, 0.0) - soft
    if mask is not None:
        log_surv = jnp.where(mask, log_surv, 0.0)
    hi, lo = _split_bf16(log_surv)
    w = jnp.dot(jnp.concatenate([hi, lo], axis=0), tri2, preferred_element_type=jnp.float32)
    within = w[:BLOCK] + w[BLOCK:]
    a = jnp.exp(log_beta + within + later)
    if mask is not None:
        a = jnp.where(mask, a, 0.0)
    pv = jnp.dot(a.astype(jnp.bfloat16), _stack_heads(vb), preferred_element_type=jnp.float32)
    tot = [jnp.sum(log_surv[:, hd * BLOCK:(hd + 1) * BLOCK], axis=-1, keepdims=True)
           for hd in range(2)]
    return pv, tot


def _sb_attn_kernel(q_ref, k_ref, v_ref, gate_ref, tri2_ref, o_ref, acc_ref, later_ref):
    step = pl.program_id(0)
    tri2 = tri2_ref[...]
    shape2 = (BLOCK, 2 * BLOCK)
    strictly_before = _lane_iota(shape2) % BLOCK < _row_iota(shape2)

    def key_rows(j):
        return pl.ds(pl.multiple_of(j * BLOCK, BLOCK), BLOCK)

    qs, more = [], []
    for sub in range(SB_Q_BLOCKS):
        i = step * SB_Q_BLOCKS + sub
        rows = slice(sub * BLOCK, (sub + 1) * BLOCK)
        diag, prev = key_rows(i), key_rows(jnp.maximum(i - 1, 0))
        no_prev = jnp.where(i > 0, 0.0, NEG_BIG)
        worst = None
        for pair in range(N_PAIR_SB):
            lanes = slice(pair * LANES, (pair + 1) * LANES)
            slot = sub * N_PAIR_SB + pair
            q = q_ref[rows, lanes]
            pv0, tot0 = _sb_tile(q, k_ref[diag, lanes], v_ref[diag, lanes], 0.0,
                                 strictly_before, tri2)
            pv1, tot1 = _sb_tile(q, k_ref[prev, lanes], v_ref[prev, lanes],
                                 _spread(tot0) + no_prev, None, tri2)
            later = _spread([tot0[0] + tot1[0], tot0[1] + tot1[1]])
            acc_ref[slot] = pv0 + pv1
            later_ref[slot] = later
            qs.append(q)
            worst = later if worst is None else jnp.maximum(worst, later)
        more.append(jnp.max(worst) > LOG_SURVIVAL_FLOOR)

    for sub in range(SB_Q_BLOCKS):
        def cond(carry):
            j, live = carry
            return jnp.logical_and(j >= 0, live)

        def body(carry, sub=sub):
            j, _ = carry
            rows = key_rows(j)
            worst = None
            for pair in range(N_PAIR_SB):
                lanes = slice(pair * LANES, (pair + 1) * LANES)
                slot = sub * N_PAIR_SB + pair
                later = later_ref[slot]
                pv, tot = _sb_tile(qs[slot], k_ref[rows, lanes], v_ref[rows, lanes], later, None, tri2)
                acc_ref[slot] += pv
                later = later + _spread(tot)
                later_ref[slot] = later
                worst = later if worst is None else jnp.maximum(worst, later)
            return j - 1, jnp.max(worst) > LOG_SURVIVAL_FLOOR

        lax.while_loop(cond, body, (step * SB_Q_BLOCKS + sub - 2, more[sub]))

    for sub in range(SB_Q_BLOCKS):
        rows = slice(sub * BLOCK, (sub + 1) * BLOCK)
        for pair in range(N_PAIR_SB):
            lanes = slice(pair * LANES, (pair + 1) * LANES)
            gate = gate_ref[rows, lanes].astype(jnp.float32)
            o_ref[rows, lanes] = (acc_ref[sub * N_PAIR_SB + pair] * gate).astype(o_ref.dtype)


def _sb_attn(qa, ka, va, gate):
    s = qa.shape[0]
    tq = SB_Q_BLOCKS * BLOCK
    blk = lambda i: (i, 0)
    resident = pl.BlockSpec((s, W_SB), lambda i: (0, 0), pipeline_mode=pl.Buffered(1))
    idx = np.arange(2 * BLOCK)
    tri2 = jnp.asarray((idx[:, None] > idx[None, :]) & (idx[:, None] // BLOCK == idx[None, :] // BLOCK),
                       dtype=jnp.bfloat16)
    n_slots = SB_Q_BLOCKS * N_PAIR_SB
    return pl.pallas_call(
        _sb_attn_kernel,
        grid=(s // tq,),
        in_specs=[pl.BlockSpec((tq, W_SB), blk), resident, resident,
                  pl.BlockSpec((tq, W_SB), blk),
                  pl.BlockSpec(tri2.shape, lambda i: (0, 0))],
        out_specs=pl.BlockSpec((tq, W_SB), blk),
        out_shape=jax.ShapeDtypeStruct((s, W_SB), jnp.bfloat16),
        scratch_shapes=[pltpu.VMEM((n_slots, BLOCK, LANES), jnp.float32),
                        pltpu.VMEM((n_slots, BLOCK, 2 * BLOCK), jnp.float32)],
        compiler_params=pltpu.CompilerParams(
            dimension_semantics=("arbitrary",), vmem_limit_bytes=VMEM_LIMIT_BYTES),
        name="sb_attn",
    )(qa, ka, va, gate, tri2)


def _band_scores(q, k_prev, k_own, bias):
    kk = jnp.concatenate([k_prev, k_own], axis=0)
    return lax.dot_general(_stack_heads(q), kk, (((1,), (1,)), ((), ())),
                           preferred_element_type=jnp.float32) + bias


def _band_softmax_pv(s_ref, p_ref, v_prev, v_own):
    vv = jnp.concatenate([v_prev, v_own], axis=0)
    vv = jnp.concatenate([vv, jnp.ones_like(vv)], axis=1)
    ms = []
    for r0 in range(0, 2 * BLOCK, SOFTMAX_ROWS):
        rows = slice(r0, r0 + SOFTMAX_ROWS)
        s = s_ref[rows, :]
        m = jnp.max(s, axis=-1, keepdims=True)
        ms.append(m)
        p_ref[rows, :] = jnp.exp2(s - m).astype(jnp.bfloat16)
    ul = jnp.dot(p_ref[...], vv, preferred_element_type=jnp.float32)

    def unstack_columns(cols):
        half = len(cols) // 2
        head0 = _lane_iota((SOFTMAX_ROWS, LANES)) < HEAD_DIM
        return jnp.concatenate(
            [jnp.where(head0, jnp.broadcast_to(cols[c], (SOFTMAX_ROWS, LANES)),
                       jnp.broadcast_to(cols[half + c], (SOFTMAX_ROWS, LANES)))
             for c in range(half)], axis=0)

    return _unstack_heads(ul[:, :LANES]), unstack_columns(ms), _unstack_heads(ul[:, LANES:])


def _dil_attn_kernel(q1_ref, k1_ref, v1_ref, k1p_ref, v1p_ref,
                     q4_ref, k4_ref, v4_ref, k4p_ref, v4p_ref,
                     q16_ref, k16_ref, v16_ref, k16p_ref, v16p_ref,
                     gate_ref, bias_ref, o_ref, out_s, max_s, den_s, score_s, prob_s):
    not_first_unit = pl.program_id(0) > 0
    blocks_per_unit = UNIT // BLOCK
    per_class4 = blocks_per_unit // 4

    def block_rows(n):
        cur = pl.ds(pl.multiple_of(n * BLOCK, BLOCK), BLOCK)
        prv = pl.ds(pl.multiple_of(jnp.maximum(n - 1, 0) * BLOCK, BLOCK), BLOCK)
        return cur, prv

    def band_bias(has_prev):
        return bias_ref[has_prev.astype(jnp.int32)]

    def scores(t, slot):
        cur, prv = block_rows(t)
        score_s[slot, 0] = _band_scores(
            q1_ref[cur, :], jnp.where(t == 0, k1p_ref[...], k1_ref[prv, :]), k1_ref[cur, :],
            band_bias(jnp.logical_or(not_first_unit, t > 0)))
        cls, n = t // per_class4, t % per_class4
        cur, prv = block_rows(n)
        score_s[slot, 1] = _band_scores(
            q4_ref[cls, cur, :], jnp.where(n == 0, k4p_ref[cls], k4_ref[cls, prv, :]),
            k4_ref[cls, cur, :], band_bias(jnp.logical_or(not_first_unit, n > 0)))
        score_s[slot, 2] = _band_scores(q16_ref[t], k16p_ref[t], k16_ref[t],
                                        band_bias(not_first_unit))

    def put(pattern, rows, res):
        u, m, l = res
        out_s[pattern, rows, :] = u
        max_s[pattern, rows, :] = m
        den_s[pattern, rows, :] = l

    def softmax_pv(t, slot):
        cur, prv = block_rows(t)
        put(0, cur, _band_softmax_pv(
            score_s.at[slot, 0], prob_s.at[slot, 0],
            jnp.where(t == 0, v1p_ref[...], v1_ref[prv, :]), v1_ref[cur, :]))
        cls, n = t // per_class4, t % per_class4
        cur, prv = block_rows(n)
        put(1, pl.ds(cls + 4 * BLOCK * n, BLOCK, stride=4), _band_softmax_pv(
            score_s.at[slot, 1], prob_s.at[slot, 1],
            jnp.where(n == 0, v4p_ref[cls], v4_ref[cls, prv, :]), v4_ref[cls, cur, :]))
        put(2, pl.ds(t, BLOCK, stride=16), _band_softmax_pv(
            score_s.at[slot, 2], prob_s.at[slot, 2], v16p_ref[t], v16_ref[t]))

    scores(0, 0)

    def body(k, carry):
        t = 2 * k
        scores(t + 1, 1)
        softmax_pv(t, 0)
        scores(jnp.minimum(t + 2, blocks_per_unit - 1), 0)
        softmax_pv(t + 1, 1)
        return carry

    lax.fori_loop(0, blocks_per_unit // 2, body, 0)

    chunk = 256
    for c0 in range(0, UNIT, chunk):
        rows = slice(c0, c0 + chunk)
        m0, m1, m2 = max_s[0, rows, :], max_s[1, rows, :], max_s[2, rows, :]
        mx = jnp.maximum(jnp.maximum(m0, m1), m2)
        w0, w1, w2 = jnp.exp2(m0 - mx), jnp.exp2(m1 - mx), jnp.exp2(m2 - mx)
        num = w0 * out_s[0, rows, :] + w1 * out_s[1, rows, :] + w2 * out_s[2, rows, :]
        den = w0 * den_s[0, rows, :] + w1 * den_s[1, rows, :] + w2 * den_s[2, rows, :]
        o_ref[rows, :] = (num / den * gate_ref[rows, :].astype(jnp.float32)).astype(o_ref.dtype)


def _band_bias_table():
    i = np.arange(2 * BLOCK)[:, None] % BLOCK
    j = np.arange(2 * BLOCK)[None, :]
    band = (j >= i) & (j <= i + BLOCK)
    table = np.stack([band & (j >= BLOCK), band])
    return jnp.asarray(np.where(table, 0.0, NEG_BIG), dtype=jnp.float32)


def _dil_attn(q1, k1, v1, q4, k4, v4, q16, k16, v16, gate):
    s = q1.shape[1]
    n_units = s // UNIT
    blocks_per_unit = UNIT // BLOCK
    n4 = UNIT // 4
    prev_unit = lambda u: jnp.maximum(u - 1, 0)
    nat = pl.BlockSpec((None, UNIT, LANES), lambda u, p: (p, u, 0))
    nat_prev = pl.BlockSpec((None, BLOCK, LANES),
                            lambda u, p: (p, jnp.maximum(u * blocks_per_unit - 1, 0), 0))
    c4 = pl.BlockSpec((None, None, 4, n4, LANES), lambda u, p: (p, u, 0, 0, 0))
    c4_prev = pl.BlockSpec((None, None, 4, BLOCK, LANES),
                           lambda u, p: (p, prev_unit(u), 0, n4 // BLOCK - 1, 0))
    c16 = pl.BlockSpec((None, None, 16, BLOCK, LANES), lambda u, p: (p, u, 0, 0, 0))
    c16_prev = pl.BlockSpec((None, None, 16, BLOCK, LANES), lambda u, p: (p, prev_unit(u), 0, 0, 0))
    bias = _band_bias_table()
    return pl.pallas_call(
        _dil_attn_kernel,
        grid=(n_units, N_PAIR_DIL),
        in_specs=[nat, nat, nat, nat_prev, nat_prev,
                  c4, c4, c4, c4_prev, c4_prev,
                  c16, c16, c16, c16_prev, c16_prev,
                  nat, pl.BlockSpec(bias.shape, lambda u, p: (0, 0, 0))],
        out_specs=nat,
        out_shape=jax.ShapeDtypeStruct((N_PAIR_DIL, s, LANES), jnp.bfloat16),
        scratch_shapes=[pltpu.VMEM((3, UNIT, LANES), jnp.float32),
                        pltpu.VMEM((3, UNIT, LANES), jnp.float32),
                        pltpu.VMEM((3, UNIT, LANES), jnp.float32),
                        pltpu.VMEM((2, 3, 2 * BLOCK, 2 * BLOCK), jnp.float32),
                        pltpu.VMEM((2, 3, 2 * BLOCK, 2 * BLOCK), jnp.bfloat16)],
        compiler_params=pltpu.CompilerParams(
            dimension_semantics=("arbitrary", "arbitrary"), vmem_limit_bytes=VMEM_LIMIT_BYTES),
        name="dil_attn",
    )(q1, k1, v1, k1, v1, q4, k4, v4, k4, v4, q16, k16, v16, k16, v16, gate, bias)


def _out_proj_kernel(x_ref, ma_ref, md_ref, w_ref, o_ref):
    o_ref[...] = x_ref[...] + _out_proj_product(ma_ref, md_ref, w_ref)


def _out_proj(layer, x, mix_a, mix_d, w_bf16):
    s, d = x.shape
    tm = ROW_TILE
    row = lambda i: (i, 0)
    return pl.pallas_call(
        _out_proj_kernel,
        grid=(s // tm,),
        in_specs=[pl.BlockSpec((tm, d), row), pl.BlockSpec((tm, W_SB), row),
                  pl.BlockSpec((N_PAIR_DIL, tm, LANES), lambda i: (0, i, 0)),
                  pl.BlockSpec((None,) + w_bf16.shape[1:], lambda i: (layer, 0, 0))],
        out_specs=pl.BlockSpec((tm, d), row),
        out_shape=jax.ShapeDtypeStruct((s, d), x.dtype),
        compiler_params=pltpu.CompilerParams(
            dimension_semantics=("arbitrary",), vmem_limit_bytes=VMEM_LIMIT_BYTES),
        name="out_proj",
    )(x, mix_a, mix_d, w_bf16)


def _rope_tables(s):
    inv_freq = 1.0 / (ROPE_THETA ** (jnp.arange(ROPE_HALF, dtype=jnp.float32) * 2.0 / ROPE_DIM))
    dim = np.arange(LANES) % HEAD_DIM
    ang = jnp.arange(s).astype(jnp.float32)[:, None] * inv_freq[dim % ROPE_HALF][None, :]
    cos, sin = jnp.cos(ang), jnp.sin(ang)
    first, second = (dim < ROPE_HALF)[None, :], ((dim >= ROPE_HALF) & (dim < ROPE_DIM))[None, :]
    cos_t = jnp.where(first | second, cos, 1.0)
    sin_a = jnp.where(first, -sin, 0.0)
    sin_b = jnp.where(second, sin, 0.0)
    return cos_t, sin_a, sin_b


def kernel(x, norm_g, w_in, q_norm_g, k_norm_g, w_out):
    b, s, d = x.shape
    assert s % UNIT == 0 and d == W_SB + W_DIL
    depth = w_in.shape[0]
    cos_t, sin_a, sin_b = _rope_tables(s)
    w_in_bf = w_in.astype(jnp.bfloat16)
    w_out_bf = w_out.astype(jnp.bfloat16)
    two = lambda g: jnp.tile(g.reshape(depth, 1, HEAD_DIM), (1, 1, LANES // HEAD_DIM))
    norm_g3, q_gain, k_gain = norm_g.reshape(depth, 1, d), two(q_norm_g), two(k_norm_g)
    outs = []
    for bi in range(b):
        xb = x[bi]
        prev = None
        for layer in range(depth):
            res = _in_proj(layer, xb, norm_g3, w_in_bf, q_gain, k_gain, cos_t, sin_a, sin_b, prev)
            if prev is not None:
                xb, res = res[0], res[1:]
            (qa, ka, va, gate_a, gate_d, q1, k1, v1, q4, k4, v4, q16, k16, v16) = res
            mix_a = _sb_attn(qa, ka, va, gate_a)
            mix_d = _dil_attn(q1, k1, v1, q4, k4, v4, q16, k16, v16, gate_d)
            prev = (mix_a, mix_d, w_out_bf)
        outs.append(_out_proj(depth - 1, xb, mix_a, mix_d, w_out_bf))
    return outs[0][None] if b == 1 else jnp.stack(outs, axis=0)
```

```python
import math

import jax
import jax.numpy as jnp
import numpy as np
from jax import lax
from jax.experimental import pallas as pl
from jax.experimental.pallas import tpu as pltpu

HEAD_DIM = 64
N_HEADS_SB = 4
N_HEADS_DIL = 12
W_SB = N_HEADS_SB * HEAD_DIM
W_DIL = N_HEADS_DIL * HEAD_DIM
DIL_PATTERNS = ((128, 1), (512, 4), (2048, 16))
ROPE_THETA = 500000.0
ROPE_DIM = HEAD_DIM // 4
ROPE_HALF = ROPE_DIM // 2
BLOCK = 128
EPS = 1e-6
QK_SCALE = 1.0 / math.sqrt(HEAD_DIM)
LOG2_E = math.log2(math.e)

LANES = 128
UNIT = 16 * BLOCK
N_PAIR_DIL = W_DIL // LANES
N_PAIR_SB = W_SB // LANES
ROW_TILE = 512
SOFTMAX_ROWS = 64
SB_Q_BLOCKS = 4
VMEM_LIMIT_BYTES = 56 * 1024 * 1024

LOG_SURVIVAL_FLOOR = -90.0
NEG_BIG = -1e30


def _split_bf16(x):
    hi = x.astype(jnp.bfloat16)
    lo = (x - hi.astype(jnp.float32)).astype(jnp.bfloat16)
    return hi, lo


def _lane_iota(shape):
    return lax.broadcasted_iota(jnp.int32, shape, 1)


def _row_iota(shape):
    return lax.broadcasted_iota(jnp.int32, shape, 0)


def _silu(x):
    return x * (1.0 / (1.0 + jnp.exp(-x)))


def _stack_heads(t):
    lane = _lane_iota(t.shape)
    zero = jnp.zeros_like(t)
    return jnp.concatenate([jnp.where(lane < HEAD_DIM, t, zero),
                            jnp.where(lane >= HEAD_DIM, t, zero)], axis=0)


def _unstack_heads(t):
    n = t.shape[0] // 2
    top, bot = t[:n], t[n:]
    return jnp.where(_lane_iota(top.shape) < HEAD_DIM, top, bot)


def _out_proj_product(ma_ref, md_ref, wo_ref):
    mix_d = jnp.concatenate([md_ref[pair] for pair in range(N_PAIR_DIL)], axis=1)
    y = jnp.dot(ma_ref[...], wo_ref[0:W_SB, :], preferred_element_type=jnp.float32)
    return y + jnp.dot(mix_d, wo_ref[W_SB:, :], preferred_element_type=jnp.float32)


def _in_proj_kernel(x_ref, *refs):
    _in_proj_body(x_ref[...], *refs)


def _out_in_proj_kernel(x_ref, ma_ref, md_ref, wo_ref, *refs):
    (g_ref, w_ref, qg_ref, kg_ref, cos_ref, sa_ref, sb_ref, xn_ref, *out_and_scratch) = refs
    x = x_ref[...] + _out_proj_product(ma_ref, md_ref, wo_ref)
    xn_ref[...] = x
    _in_proj_body(x, g_ref, w_ref, qg_ref, kg_ref, cos_ref, sa_ref, sb_ref, *out_and_scratch)


def _in_proj_body(x, g_ref, w_ref, qg_ref, kg_ref, cos_ref, sa_ref, sb_ref,
                  qa_ref, ka_ref, va_ref, gate_a_ref, gate_d_ref,
                  q1_ref, k1_ref, v1_ref, q4_ref, k4_ref, v4_ref, q16_ref, k16_ref, v16_ref,
                  slab_ref, slab4_ref):
    tm = x.shape[0]
    ms = jnp.mean(x * x, axis=-1, keepdims=True)
    h = (x * lax.rsqrt(ms + EPS) * g_ref[...]).astype(jnp.bfloat16)

    def proj(c0, width):
        return jnp.dot(h, w_ref[:, c0:c0 + width], preferred_element_type=jnp.float32)

    base = 4 * W_SB

    r = (_row_iota((2 * LANES, LANES)) % LANES) // HEAD_DIM
    c = _lane_iota((2 * LANES, LANES)) // HEAD_DIM
    head_mean = jnp.where(r == c, 1.0 / HEAD_DIM, 0.0).astype(jnp.bfloat16)

    cos = cos_ref[...]
    sin_a = sa_ref[...]
    sin_b = sb_ref[...]

    def emit(pair, t, o1_ref, o4_ref, o16_ref):
        o1_ref[pair] = t.astype(jnp.bfloat16)
        slab_ref[pair] = t
        n4 = tm // 4
        for c4 in range(4):
            t4 = slab_ref[pair, pl.ds(c4, n4, stride=4), :]
            o4_ref[pair, c4] = t4.astype(jnp.bfloat16)
            slab4_ref[pair, c4 * n4:(c4 + 1) * n4, :] = t4
        n16 = tm // 16
        for c4 in range(4):
            for b in range(4):
                t16 = slab4_ref[pair, pl.ds(c4 * n4 + b, n16, stride=4), :]
                o16_ref[pair, c4 + 4 * b] = t16.astype(jnp.bfloat16)

    def qk_norm_rope(t, gain, scale):
        hi, lo = _split_bf16(t * t)
        msq = jnp.dot(jnp.concatenate([hi, lo], axis=1), head_mean,
                      preferred_element_type=jnp.float32)
        t = t * lax.rsqrt(msq + EPS) * gain
        up = pltpu.roll(t, LANES - ROPE_HALF, axis=1)
        down = pltpu.roll(t, ROPE_HALF, axis=1)
        t = t * cos + up * sin_a + down * sin_b
        if scale != 1.0:
            t = t * scale
        return t

    pq = proj(base, W_DIL)
    for pair in range(N_PAIR_DIL):
        t = qk_norm_rope(pq[:, pair * LANES:(pair + 1) * LANES], qg_ref[...], QK_SCALE * LOG2_E)
        emit(pair, t, q1_ref, q4_ref, q16_ref)
    pk = proj(base + W_DIL, W_DIL)
    for pair in range(N_PAIR_DIL):
        t = qk_norm_rope(pk[:, pair * LANES:(pair + 1) * LANES], kg_ref[...], 1.0)
        emit(pair, t, k1_ref, k4_ref, k16_ref)
    pv = proj(base + 2 * W_DIL, W_DIL)
    for pair in range(N_PAIR_DIL):
        emit(pair, pv[:, pair * LANES:(pair + 1) * LANES], v1_ref, v4_ref, v16_ref)

    pa = proj(0, 4 * W_SB)
    qa_ref[...] = (pa[:, 0:W_SB] * QK_SCALE).astype(jnp.bfloat16)
    ka_ref[...] = pa[:, W_SB:2 * W_SB].astype(jnp.bfloat16)
    va_ref[...] = pa[:, 2 * W_SB:3 * W_SB].astype(jnp.bfloat16)
    gate_a_ref[...] = _silu(pa[:, 3 * W_SB:]).astype(jnp.bfloat16)
    gate_d = _silu(proj(base + 3 * W_DIL, W_DIL)).astype(jnp.bfloat16)
    for pair in range(N_PAIR_DIL):
        gate_d_ref[pair] = gate_d[:, pair * LANES:(pair + 1) * LANES]


def _in_proj(layer, x, g, w_bf16, qg, kg, cos_t, sin_a, sin_b, prev=None):
    s, d = x.shape
    tm = ROW_TILE
    per_unit = UNIT // tm
    n_units = s // UNIT
    bf = jnp.bfloat16
    row = lambda i: (i, 0)
    nat_spec = pl.BlockSpec((N_PAIR_DIL, tm, LANES), lambda i: (0, i, 0))
    c4_spec = pl.BlockSpec((N_PAIR_DIL, None, 4, tm // 4, LANES),
                           lambda i: (0, i // per_unit, 0, i % per_unit, 0))
    c16_spec = pl.BlockSpec((N_PAIR_DIL, None, 16, tm // 16, LANES),
                            lambda i: (0, i // per_unit, 0, i % per_unit, 0))
    nat_shape = jax.ShapeDtypeStruct((N_PAIR_DIL, s, LANES), bf)
    c4_shape = jax.ShapeDtypeStruct((N_PAIR_DIL, n_units, 4, UNIT // 4, LANES), bf)
    c16_shape = jax.ShapeDtypeStruct((N_PAIR_DIL, n_units, 16, UNIT // 16, LANES), bf)
    sb_shape = jax.ShapeDtypeStruct((s, W_SB), bf)
    in_specs = [
        pl.BlockSpec((None, 1, d), lambda i: (layer, 0, 0)),
        pl.BlockSpec((None,) + w_bf16.shape[1:], lambda i: (layer, 0, 0)),
        pl.BlockSpec((None, 1, LANES), lambda i: (layer, 0, 0)),
        pl.BlockSpec((None, 1, LANES), lambda i: (layer, 0, 0)),
        pl.BlockSpec((tm, LANES), row),
        pl.BlockSpec((tm, LANES), row),
        pl.BlockSpec((tm, LANES), row),
    ]
    operands = [g, w_bf16, qg, kg, cos_t, sin_a, sin_b]
    out_specs = [
        pl.BlockSpec((tm, W_SB), row), pl.BlockSpec((tm, W_SB), row),
        pl.BlockSpec((tm, W_SB), row), pl.BlockSpec((tm, W_SB), row), nat_spec,
        nat_spec, nat_spec, nat_spec, c4_spec, c4_spec, c4_spec,
        c16_spec, c16_spec, c16_spec,
    ]
    out_shape = [sb_shape, sb_shape, sb_shape, sb_shape, nat_shape,
                 nat_shape, nat_shape, nat_shape, c4_shape, c4_shape, c4_shape,
                 c16_shape, c16_shape, c16_shape]
    x_spec = pl.BlockSpec((tm, d), row)
    if prev is None:
        body, name = _in_proj_kernel, "in_proj"
        in_specs = [x_spec] + in_specs
        operands = [x] + operands
    else:
        mix_a, mix_d, w_out = prev
        body, name = _out_in_proj_kernel, "out_in_proj"
        in_specs = [x_spec, pl.BlockSpec((tm, W_SB), row), nat_spec,
                    pl.BlockSpec((None,) + w_out.shape[1:], lambda i: (layer - 1, 0, 0))] + in_specs
        operands = [x, mix_a, mix_d, w_out] + operands
        out_specs = [x_spec] + out_specs
        out_shape = [jax.ShapeDtypeStruct((s, d), x.dtype)] + out_shape
    return pl.pallas_call(
        body,
        grid=(s // tm,),
        in_specs=in_specs,
        out_specs=out_specs,
        out_shape=out_shape,
        scratch_shapes=[pltpu.VMEM((N_PAIR_DIL, tm, LANES), jnp.float32),
                        pltpu.VMEM((N_PAIR_DIL, tm, LANES), jnp.float32)],
        compiler_params=pltpu.CompilerParams(
            dimension_semantics=("arbitrary",), vmem_limit_bytes=VMEM_LIMIT_BYTES),
        name=name,
    )(*operands)


def _spread(per_head):
    return jnp.concatenate([jnp.broadcast_to(t, (BLOCK, BLOCK)) for t in per_head], axis=1)


def _sb_tile(q, kb, vb, later, mask, tri2):
    z = lax.dot_general(q, _stack_heads(kb), (((1,), (1,)), ((), ())),
                        preferred_element_type=jnp.float32)
    soft = jnp.log(1.0 + jnp.exp(-jnp.abs(z)))
    log_surv = -(jnp.maximum(z, 0.0) + soft)
    log_beta = jnp.minimum(z, 0.0) - soft
    if mask is not None:
        log_surv = jnp.where(mask, log_surv, 0.0)
    hi, lo = _split_bf16(log_surv)
    w = jnp.dot(jnp.concatenate([hi, lo], axis=0), tri2, preferred_element_type=jnp.float32)
    within = w[:BLOCK] + w[BLOCK:]
    a = jnp.exp(log_beta + within + later)
    if mask is not None:
        a = jnp.where(mask, a, 0.0)
    pv = jnp.dot(a.astype(jnp.bfloat16), _stack_heads(vb), preferred_element_type=jnp.float32)
    tot = [jnp.sum(log_surv[:, hd * BLOCK:(hd + 1) * BLOCK], axis=-1, keepdims=True)
           for hd in range(2)]
    return pv, tot


def _sb_attn_kernel(q_ref, k_ref, v_ref, gate_ref, tri2_ref, o_ref, acc_ref, later_ref):
    step = pl.program_id(0)
    tri2 = tri2_ref[...]
    shape2 = (BLOCK, 2 * BLOCK)
    strictly_before = _lane_iota(shape2) % BLOCK < _row_iota(shape2)

    def key_rows(j):
        return pl.ds(pl.multiple_of(j * BLOCK, BLOCK), BLOCK)

    qs, more = [], []
    for sub in range(SB_Q_BLOCKS):
        i = step * SB_Q_BLOCKS + sub
        rows = slice(sub * BLOCK, (sub + 1) * BLOCK)
        diag, prev = key_rows(i), key_rows(jnp.maximum(i - 1, 0))
        no_prev = jnp.where(i > 0, 0.0, NEG_BIG)
        worst = None
        for pair in range(N_PAIR_SB):
            lanes = slice(pair * LANES, (pair + 1) * LANES)
            slot = sub * N_PAIR_SB + pair
            q = q_ref[rows, lanes]
            pv0, tot0 = _sb_tile(q, k_ref[diag, lanes], v_ref[diag, lanes], 0.0,
                                 strictly_before, tri2)
            pv1, tot1 = _sb_tile(q, k_ref[prev, lanes], v_ref[prev, lanes],
                                 _spread(tot0) + no_prev, None, tri2)
            later = _spread([tot0[0] + tot1[0], tot0[1] + tot1[1]])
            acc_ref[slot] = pv0 + pv1
            later_ref[slot] = later
            qs.append(q)
            worst = later if worst is None else jnp.maximum(worst, later)
        more.append(jnp.max(worst) > LOG_SURVIVAL_FLOOR)

    for sub in range(SB_Q_BLOCKS):
        def cond(carry):
            j, live = carry
            return jnp.logical_and(j >= 0, live)

        def body(carry, sub=sub):
            j, _ = carry
            rows = key_rows(j)
            worst = None
            for pair in range(N_PAIR_SB):
                lanes = slice(pair * LANES, (pair + 1) * LANES)
                slot = sub * N_PAIR_SB + pair
                later = later_ref[slot]
                pv, tot = _sb_tile(qs[slot], k_ref[rows, lanes], v_ref[rows, lanes], later, None, tri2)
                acc_ref[slot] += pv
                later = later + _spread(tot)
                later_ref[slot] = later
                worst = later if worst is None else jnp.maximum(worst, later)
            return j - 1, jnp.max(worst) > LOG_SURVIVAL_FLOOR

        lax.while_loop(cond, body, (step * SB_Q_BLOCKS + sub - 2, more[sub]))

    for sub in range(SB_Q_BLOCKS):
        rows = slice(sub * BLOCK, (sub + 1) * BLOCK)
        for pair in range(N_PAIR_SB):
            lanes = slice(pair * LANES, (pair + 1) * LANES)
            gate = gate_ref[rows, lanes].astype(jnp.float32)
            o_ref[rows, lanes] = (acc_ref[sub * N_PAIR_SB + pair] * gate).astype(o_ref.dtype)


def _sb_attn(qa, ka, va, gate):
    s = qa.shape[0]
    tq = SB_Q_BLOCKS * BLOCK
    blk = lambda i: (i, 0)
    resident = pl.BlockSpec((s, W_SB), lambda i: (0, 0), pipeline_mode=pl.Buffered(1))
    idx = np.arange(2 * BLOCK)
    tri2 = jnp.asarray((idx[:, None] > idx[None, :]) & (idx[:, None] // BLOCK == idx[None, :] // BLOCK),
                       dtype=jnp.bfloat16)
    n_slots = SB_Q_BLOCKS * N_PAIR_SB
    return pl.pallas_call(
        _sb_attn_kernel,
        grid=(s // tq,),
        in_specs=[pl.BlockSpec((tq, W_SB), blk), resident, resident,
                  pl.BlockSpec((tq, W_SB), blk),
                  pl.BlockSpec(tri2.shape, lambda i: (0, 0))],
        out_specs=pl.BlockSpec((tq, W_SB), blk),
        out_shape=jax.ShapeDtypeStruct((s, W_SB), jnp.bfloat16),
        scratch_shapes=[pltpu.VMEM((n_slots, BLOCK, LANES), jnp.float32),
                        pltpu.VMEM((n_slots, BLOCK, 2 * BLOCK), jnp.float32)],
        compiler_params=pltpu.CompilerParams(
            dimension_semantics=("arbitrary",), vmem_limit_bytes=VMEM_LIMIT_BYTES),
        name="sb_attn",
    )(qa, ka, va, gate, tri2)


def _band_scores(q, k_prev, k_own, bias):
    kk = jnp.concatenate([k_prev, k_own], axis=0)
    return lax.dot_general(_stack_heads(q), kk, (((1,), (1,)), ((), ())),
                           preferred_element_type=jnp.float32) + bias


def _band_softmax_pv(s_ref, p_ref, v_prev, v_own):
    vv = jnp.concatenate([v_prev, v_own], axis=0)
    vv = jnp.concatenate([vv, jnp.ones_like(vv)], axis=1)
    ms = []
    for r0 in range(0, 2 * BLOCK, SOFTMAX_ROWS):
        rows = slice(r0, r0 + SOFTMAX_ROWS)
        s = s_ref[rows, :]
        m = jnp.max(s, axis=-1, keepdims=True)
        ms.append(m)
        p_ref[rows, :] = jnp.exp2(s - m).astype(jnp.bfloat16)
    ul = jnp.dot(p_ref[...], vv, preferred_element_type=jnp.float32)

    def unstack_columns(cols):
        half = len(cols) // 2
        head0 = _lane_iota((SOFTMAX_ROWS, LANES)) < HEAD_DIM
        return jnp.concatenate(
            [jnp.where(head0, jnp.broadcast_to(cols[c], (SOFTMAX_ROWS, LANES)),
                       jnp.broadcast_to(cols[half + c], (SOFTMAX_ROWS, LANES)))
             for c in range(half)], axis=0)

    return _unstack_heads(ul[:, :LANES]), unstack_columns(ms), _unstack_heads(ul[:, LANES:])


def _dil_attn_kernel(q1_ref, k1_ref, v1_ref, k1p_ref, v1p_ref,
                     q4_ref, k4_ref, v4_ref, k4p_ref, v4p_ref,
                     q16_ref, k16_ref, v16_ref, k16p_ref, v16p_ref,
                     gate_ref, bias_ref, o_ref, out_s, max_s, den_s, score_s, prob_s, stage_s):
    not_first_unit = pl.program_id(0) > 0
    blocks_per_unit = UNIT // BLOCK
    per_class4 = blocks_per_unit // 4

    def block_rows(n):
        cur = pl.ds(pl.multiple_of(n * BLOCK, BLOCK), BLOCK)
        prv = pl.ds(pl.multiple_of(jnp.maximum(n - 1, 0) * BLOCK, BLOCK), BLOCK)
        return cur, prv

    def band_bias(has_prev):
        return bias_ref[has_prev.astype(jnp.int32)]

    def scores(t, slot):
        cur, prv = block_rows(t)
        score_s[slot, 0] = _band_scores(
            q1_ref[cur, :], jnp.where(t == 0, k1p_ref[...], k1_ref[prv, :]), k1_ref[cur, :],
            band_bias(jnp.logical_or(not_first_unit, t > 0)))
        cls, n = t // per_class4, t % per_class4
        cur, prv = block_rows(n)
        score_s[slot, 1] = _band_scores(
            q4_ref[cls, cur, :], jnp.where(n == 0, k4p_ref[cls], k4_ref[cls, prv, :]),
            k4_ref[cls, cur, :], band_bias(jnp.logical_or(not_first_unit, n > 0)))
        score_s[slot, 2] = _band_scores(q16_ref[t], k16p_ref[t], k16_ref[t],
                                        band_bias(not_first_unit))

    def put(pattern, rows, res):
        u, m, l = res
        out_s[pattern, rows, :] = u
        max_s[pattern, rows, :] = m
        den_s[pattern, rows, :] = l

    def softmax_pv(t, slot):
        cur, prv = block_rows(t)
        put(0, cur, _band_softmax_pv(
            score_s.at[slot, 0], prob_s.at[slot, 0],
            jnp.where(t == 0, v1p_ref[...], v1_ref[prv, :]), v1_ref[cur, :]))
        cls, n = t // per_class4, t % per_class4
        cur, prv = block_rows(n)
        put(1, pl.ds(cls + 4 * BLOCK * n, BLOCK, stride=4), _band_softmax_pv(
            score_s.at[slot, 1], prob_s.at[slot, 1],
            jnp.where(n == 0, v4p_ref[cls], v4_ref[cls, prv, :]), v4_ref[cls, cur, :]))
        res16 = _band_softmax_pv(score_s.at[slot, 2], prob_s.at[slot, 2], v16p_ref[t], v16_ref[t])
        for kind, val in enumerate(res16):
            stage_s[kind, t % 4, pl.ds(t // 4, BLOCK, stride=4), :] = val

    scores(0, 0)

    def body(k, carry):
        t = 2 * k
        scores(t + 1, 1)
        softmax_pv(t, 0)
        scores(jnp.minimum(t + 2, blocks_per_unit - 1), 0)
        softmax_pv(t + 1, 1)
        return carry

    lax.fori_loop(0, blocks_per_unit // 2, body, 0)

    for c4 in range(4):
        for piece in range(UNIT // 4 // BLOCK):
            src = slice(piece * BLOCK, (piece + 1) * BLOCK)
            rows = pl.ds(c4 + 4 * BLOCK * piece, BLOCK, stride=4)
            out_s[2, rows, :] = stage_s[0, c4, src, :]
            max_s[2, rows, :] = stage_s[1, c4, src, :]
            den_s[2, rows, :] = stage_s[2, c4, src, :]

    chunk = 256
    for c0 in range(0, UNIT, chunk):
        rows = slice(c0, c0 + chunk)
        m0, m1, m2 = max_s[0, rows, :], max_s[1, rows, :], max_s[2, rows, :]
        mx = jnp.maximum(jnp.maximum(m0, m1), m2)
        w0, w1, w2 = jnp.exp2(m0 - mx), jnp.exp2(m1 - mx), jnp.exp2(m2 - mx)
        num = w0 * out_s[0, rows, :] + w1 * out_s[1, rows, :] + w2 * out_s[2, rows, :]
        den = w0 * den_s[0, rows, :] + w1 * den_s[1, rows, :] + w2 * den_s[2, rows, :]
        o_ref[rows, :] = (num / den * gate_ref[rows, :].astype(jnp.float32)).astype(o_ref.dtype)


def _band_bias_table():
    i = np.arange(2 * BLOCK)[:, None] % BLOCK
    j = np.arange(2 * BLOCK)[None, :]
    band = (j >= i) & (j <= i + BLOCK)
    table = np.stack([band & (j >= BLOCK), band])
    return jnp.asarray(np.where(table, 0.0, NEG_BIG), dtype=jnp.float32)


def _dil_attn(q1, k1, v1, q4, k4, v4, q16, k16, v16, gate):
    s = q1.shape[1]
    n_units = s // UNIT
    blocks_per_unit = UNIT // BLOCK
    n4 = UNIT // 4
    prev_unit = lambda u: jnp.maximum(u - 1, 0)
    nat = pl.BlockSpec((None, UNIT, LANES), lambda u, p: (p, u, 0))
    nat_prev = pl.BlockSpec((None, BLOCK, LANES),
                            lambda u, p: (p, jnp.maximum(u * blocks_per_unit - 1, 0), 0))
    c4 = pl.BlockSpec((None, None, 4, n4, LANES), lambda u, p: (p, u, 0, 0, 0))
    c4_prev = pl.BlockSpec((None, None, 4, BLOCK, LANES),
                           lambda u, p: (p, prev_unit(u), 0, n4 // BLOCK - 1, 0))
    c16 = pl.BlockSpec((None, None, 16, BLOCK, LANES), lambda u, p: (p, u, 0, 0, 0))
    c16_prev = pl.BlockSpec((None, None, 16, BLOCK, LANES), lambda u, p: (p, prev_unit(u), 0, 0, 0))
    bias = _band_bias_table()
    return pl.pallas_call(
        _dil_attn_kernel,
        grid=(n_units, N_PAIR_DIL),
        in_specs=[nat, nat, nat, nat_prev, nat_prev,
                  c4, c4, c4, c4_prev, c4_prev,
                  c16, c16, c16, c16_prev, c16_prev,
                  nat, pl.BlockSpec(bias.shape, lambda u, p: (0, 0, 0))],
        out_specs=nat,
        out_shape=jax.ShapeDtypeStruct((N_PAIR_DIL, s, LANES), jnp.bfloat16),
        scratch_shapes=[pltpu.VMEM((3, UNIT, LANES), jnp.float32),
                        pltpu.VMEM((3, UNIT, LANES), jnp.float32),
                        pltpu.VMEM((3, UNIT, LANES), jnp.float32),
                        pltpu.VMEM((2, 3, 2 * BLOCK, 2 * BLOCK), jnp.float32),
                        pltpu.VMEM((2, 3, 2 * BLOCK, 2 * BLOCK), jnp.bfloat16),
                        pltpu.VMEM((3, 4, UNIT // 4, LANES), jnp.float32)],
        compiler_params=pltpu.CompilerParams(
            dimension_semantics=("arbitrary", "arbitrary"), vmem_limit_bytes=VMEM_LIMIT_BYTES),
        name="dil_attn",
    )(q1, k1, v1, k1, v1, q4, k4, v4, k4, v4, q16, k16, v16, k16, v16, gate, bias)


def _out_proj_kernel(x_ref, ma_ref, md_ref, w_ref, o_ref):
    o_ref[...] = x_ref[...] + _out_proj_product(ma_ref, md_ref, w_ref)


def _out_proj(layer, x, mix_a, mix_d, w_bf16):
    s, d = x.shape
    tm = ROW_TILE
    row = lambda i: (i, 0)
    return pl.pallas_call(
        _out_proj_kernel,
        grid=(s // tm,),
        in_specs=[pl.BlockSpec((tm, d), row), pl.BlockSpec((tm, W_SB), row),
                  pl.BlockSpec((N_PAIR_DIL, tm, LANES), lambda i: (0, i, 0)),
                  pl.BlockSpec((None,) + w_bf16.shape[1:], lambda i: (layer, 0, 0))],
        out_specs=pl.BlockSpec((tm, d), row),
        out_shape=jax.ShapeDtypeStruct((s, d), x.dtype),
        compiler_params=pltpu.CompilerParams(
            dimension_semantics=("arbitrary",), vmem_limit_bytes=VMEM_LIMIT_BYTES),
        name="out_proj",
    )(x, mix_a, mix_d, w_bf16)


def _rope_tables(s):
    inv_freq = 1.0 / (ROPE_THETA ** (jnp.arange(ROPE_HALF, dtype=jnp.float32) * 2.0 / ROPE_DIM))
    dim = np.arange(LANES) % HEAD_DIM
    ang = jnp.arange(s).astype(jnp.float32)[:, None] * inv_freq[dim % ROPE_HALF][None, :]
    cos, sin = jnp.cos(ang), jnp.sin(ang)
    first, second = (dim < ROPE_HALF)[None, :], ((dim >= ROPE_HALF) & (dim < ROPE_DIM))[None, :]
    cos_t = jnp.where(first | second, cos, 1.0)
    sin_a = jnp.where(first, -sin, 0.0)
    sin_b = jnp.where(second, sin, 0.0)
    return cos_t, sin_a, sin_b


def kernel(x, norm_g, w_in, q_norm_g, k_norm_g, w_out):
    b, s, d = x.shape
    assert s % UNIT == 0 and d == W_SB + W_DIL
    depth = w_in.shape[0]
    cos_t, sin_a, sin_b = _rope_tables(s)
    w_in_bf = w_in.astype(jnp.bfloat16)
    w_out_bf = w_out.astype(jnp.bfloat16)
    two = lambda g: jnp.tile(g.reshape(depth, 1, HEAD_DIM), (1, 1, LANES // HEAD_DIM))
    norm_g3, q_gain, k_gain = norm_g.reshape(depth, 1, d), two(q_norm_g), two(k_norm_g)
    outs = []
    for bi in range(b):
        xb = x[bi]
        prev = None
        for layer in range(depth):
            res = _in_proj(layer, xb, norm_g3, w_in_bf, q_gain, k_gain, cos_t, sin_a, sin_b, prev)
            if prev is not None:
                xb, res = res[0], res[1:]
            (qa, ka, va, gate_a, gate_d, q1, k1, v1, q4, k4, v4, q16, k16, v16) = res
            mix_a = _sb_attn(qa, ka, va, gate_a)
            mix_d = _dil_attn(q1, k1, v1, q4, k4, v4, q16, k16, v16, gate_d)
            prev = (mix_a, mix_d, w_out_bf)
        outs.append(_out_proj(depth - 1, xb, mix_a, mix_d, w_out_bf))
    return outs[0][None] if b == 1 else jnp.stack(outs, axis=0)
```

```python
import math

import jax
import jax.numpy as jnp
import numpy as np
from jax import lax
from jax.experimental import pallas as pl
from jax.experimental.pallas import tpu as pltpu

HEAD_DIM = 64
N_HEADS_SB = 4
N_HEADS_DIL = 12
W_SB = N_HEADS_SB * HEAD_DIM
W_DIL = N_HEADS_DIL * HEAD_DIM
DIL_PATTERNS = ((128, 1), (512, 4), (2048, 16))
ROPE_THETA = 500000.0
ROPE_DIM = HEAD_DIM // 4
ROPE_HALF = ROPE_DIM // 2
BLOCK = 128
EPS = 1e-6
QK_SCALE = 1.0 / math.sqrt(HEAD_DIM)
LOG2_E = math.log2(math.e)

LANES = 128
UNIT = 16 * BLOCK
N_PAIR_DIL = W_DIL // LANES
N_PAIR_SB = W_SB // LANES
ROW_TILE = 512
SOFTMAX_ROWS = 64
SB_Q_BLOCKS = 4
PIPELINE_SLOTS = 2
VMEM_LIMIT_BYTES = 56 * 1024 * 1024

LOG_SURVIVAL_FLOOR = -90.0 * LOG2_E
NEG_BIG = -1e30


def _split_bf16(x):
    hi = x.astype(jnp.bfloat16)
    lo = (x - hi.astype(jnp.float32)).astype(jnp.bfloat16)
    return hi, lo


def _lane_iota(shape):
    return lax.broadcasted_iota(jnp.int32, shape, 1)


def _row_iota(shape):
    return lax.broadcasted_iota(jnp.int32, shape, 0)


def _silu(x):
    return x * (1.0 / (1.0 + jnp.exp(-x)))


def _stack_heads(t):
    lane = _lane_iota(t.shape)
    zero = jnp.zeros_like(t)
    return jnp.concatenate([jnp.where(lane < HEAD_DIM, t, zero),
                            jnp.where(lane >= HEAD_DIM, t, zero)], axis=0)


def _unstack_heads(t):
    n = t.shape[0] // 2
    top, bot = t[:n], t[n:]
    return jnp.where(_lane_iota(top.shape) < HEAD_DIM, top, bot)


def _out_proj_product(ma_ref, md_ref, wo_ref):
    mix_d = jnp.concatenate([md_ref[pair] for pair in range(N_PAIR_DIL)], axis=1)
    y = jnp.dot(ma_ref[...], wo_ref[0:W_SB, :], preferred_element_type=jnp.float32)
    return y + jnp.dot(mix_d, wo_ref[W_SB:, :], preferred_element_type=jnp.float32)


def _in_proj_kernel(x_ref, *refs):
    _in_proj_body(x_ref[...], *refs)


def _out_in_proj_kernel(x_ref, ma_ref, md_ref, wo_ref, *refs):
    (g_ref, w_ref, qg_ref, kg_ref, cos_ref, sa_ref, sb_ref, xn_ref, *out_and_scratch) = refs
    x = x_ref[...] + _out_proj_product(ma_ref, md_ref, wo_ref)
    xn_ref[...] = x
    _in_proj_body(x, g_ref, w_ref, qg_ref, kg_ref, cos_ref, sa_ref, sb_ref, *out_and_scratch)


def _in_proj_body(x, g_ref, w_ref, qg_ref, kg_ref, cos_ref, sa_ref, sb_ref,
                  qa_ref, ka_ref, va_ref, gate_a_ref, gate_d_ref,
                  q1_ref, k1_ref, v1_ref, q4_ref, k4_ref, v4_ref, q16_ref, k16_ref, v16_ref,
                  slab_ref, slab4_ref):
    tm = x.shape[0]
    ms = jnp.mean(x * x, axis=-1, keepdims=True)
    h = (x * lax.rsqrt(ms + EPS) * g_ref[...]).astype(jnp.bfloat16)

    def proj(c0, width):
        return jnp.dot(h, w_ref[:, c0:c0 + width], preferred_element_type=jnp.float32)

    base = 4 * W_SB

    r = (_row_iota((2 * LANES, LANES)) % LANES) // HEAD_DIM
    c = _lane_iota((2 * LANES, LANES)) // HEAD_DIM
    head_mean = jnp.where(r == c, 1.0 / HEAD_DIM, 0.0).astype(jnp.bfloat16)

    cos = cos_ref[...]
    sin_a = sa_ref[...]
    sin_b = sb_ref[...]

    def emit(pair, t, o1_ref, o4_ref, o16_ref):
        o1_ref[pair] = t.astype(jnp.bfloat16)
        slab_ref[pair] = t
        n4 = tm // 4
        for c4 in range(4):
            t4 = slab_ref[pair, pl.ds(c4, n4, stride=4), :]
            o4_ref[pair, c4] = t4.astype(jnp.bfloat16)
            slab4_ref[pair, c4 * n4:(c4 + 1) * n4, :] = t4
        n16 = tm // 16
        for c4 in range(4):
            for b in range(4):
                t16 = slab4_ref[pair, pl.ds(c4 * n4 + b, n16, stride=4), :]
                o16_ref[pair, c4 + 4 * b] = t16.astype(jnp.bfloat16)

    def qk_norm_rope(t, gain, scale):
        hi, lo = _split_bf16(t * t)
        msq = jnp.dot(jnp.concatenate([hi, lo], axis=1), head_mean,
                      preferred_element_type=jnp.float32)
        t = t * lax.rsqrt(msq + EPS) * gain
        up = pltpu.roll(t, LANES - ROPE_HALF, axis=1)
        down = pltpu.roll(t, ROPE_HALF, axis=1)
        t = t * cos + up * sin_a + down * sin_b
        if scale != 1.0:
            t = t * scale
        return t

    pq = proj(base, W_DIL)
    for pair in range(N_PAIR_DIL):
        t = qk_norm_rope(pq[:, pair * LANES:(pair + 1) * LANES], qg_ref[...], QK_SCALE * LOG2_E)
        emit(pair, t, q1_ref, q4_ref, q16_ref)
    pk = proj(base + W_DIL, W_DIL)
    for pair in range(N_PAIR_DIL):
        t = qk_norm_rope(pk[:, pair * LANES:(pair + 1) * LANES], kg_ref[...], 1.0)
        emit(pair, t, k1_ref, k4_ref, k16_ref)
    pv = proj(base + 2 * W_DIL, W_DIL)
    for pair in range(N_PAIR_DIL):
        emit(pair, pv[:, pair * LANES:(pair + 1) * LANES], v1_ref, v4_ref, v16_ref)

    pa = proj(0, 4 * W_SB)
    qa_ref[...] = (pa[:, 0:W_SB] * (QK_SCALE * LOG2_E)).astype(jnp.bfloat16)
    ka_ref[...] = pa[:, W_SB:2 * W_SB].astype(jnp.bfloat16)
    va_ref[...] = pa[:, 2 * W_SB:3 * W_SB].astype(jnp.bfloat16)
    gate_a_ref[...] = _silu(pa[:, 3 * W_SB:]).astype(jnp.bfloat16)
    gate_d = _silu(proj(base + 3 * W_DIL, W_DIL)).astype(jnp.bfloat16)
    for pair in range(N_PAIR_DIL):
        gate_d_ref[pair] = gate_d[:, pair * LANES:(pair + 1) * LANES]


def _in_proj(layer, x, g, w_bf16, qg, kg, cos_t, sin_a, sin_b, prev=None):
    s, d = x.shape
    tm = ROW_TILE
    per_unit = UNIT // tm
    n_units = s // UNIT
    bf = jnp.bfloat16
    row = lambda i: (i, 0)
    nat_spec = pl.BlockSpec((N_PAIR_DIL, tm, LANES), lambda i: (0, i, 0))
    c4_spec = pl.BlockSpec((N_PAIR_DIL, None, 4, tm // 4, LANES),
                           lambda i: (0, i // per_unit, 0, i % per_unit, 0))
    c16_spec = pl.BlockSpec((N_PAIR_DIL, None, 16, tm // 16, LANES),
                            lambda i: (0, i // per_unit, 0, i % per_unit, 0))
    nat_shape = jax.ShapeDtypeStruct((N_PAIR_DIL, s, LANES), bf)
    c4_shape = jax.ShapeDtypeStruct((N_PAIR_DIL, n_units, 4, UNIT // 4, LANES), bf)
    c16_shape = jax.ShapeDtypeStruct((N_PAIR_DIL, n_units, 16, UNIT // 16, LANES), bf)
    sb_shape = jax.ShapeDtypeStruct((s, W_SB), bf)
    in_specs = [
        pl.BlockSpec((None, 1, d), lambda i: (layer, 0, 0)),
        pl.BlockSpec((None,) + w_bf16.shape[1:], lambda i: (layer, 0, 0)),
        pl.BlockSpec((None, 1, LANES), lambda i: (layer, 0, 0)),
        pl.BlockSpec((None, 1, LANES), lambda i: (layer, 0, 0)),
        pl.BlockSpec((tm, LANES), row),
        pl.BlockSpec((tm, LANES), row),
        pl.BlockSpec((tm, LANES), row),
    ]
    operands = [g, w_bf16, qg, kg, cos_t, sin_a, sin_b]
    out_specs = [
        pl.BlockSpec((tm, W_SB), row), pl.BlockSpec((tm, W_SB), row),
        pl.BlockSpec((tm, W_SB), row), pl.BlockSpec((tm, W_SB), row), nat_spec,
        nat_spec, nat_spec, nat_spec, c4_spec, c4_spec, c4_spec,
        c16_spec, c16_spec, c16_spec,
    ]
    out_shape = [sb_shape, sb_shape, sb_shape, sb_shape, nat_shape,
                 nat_shape, nat_shape, nat_shape, c4_shape, c4_shape, c4_shape,
                 c16_shape, c16_shape, c16_shape]
    x_spec = pl.BlockSpec((tm, d), row)
    if prev is None:
        body, name = _in_proj_kernel, "in_proj"
        in_specs = [x_spec] + in_specs
        operands = [x] + operands
    else:
        mix_a, mix_d, w_out = prev
        body, name = _out_in_proj_kernel, "out_in_proj"
        in_specs = [x_spec, pl.BlockSpec((tm, W_SB), row), nat_spec,
                    pl.BlockSpec((None,) + w_out.shape[1:], lambda i: (layer - 1, 0, 0))] + in_specs
        operands = [x, mix_a, mix_d, w_out] + operands
        out_specs = [x_spec] + out_specs
        out_shape = [jax.ShapeDtypeStruct((s, d), x.dtype)] + out_shape
    return pl.pallas_call(
        body,
        grid=(s // tm,),
        in_specs=in_specs,
        out_specs=out_specs,
        out_shape=out_shape,
        scratch_shapes=[pltpu.VMEM((N_PAIR_DIL, tm, LANES), jnp.float32),
                        pltpu.VMEM((N_PAIR_DIL, tm, LANES), jnp.float32)],
        compiler_params=pltpu.CompilerParams(
            dimension_semantics=("arbitrary",), vmem_limit_bytes=VMEM_LIMIT_BYTES),
        name=name,
    )(*operands)


def _spread(per_head):
    return jnp.concatenate([jnp.broadcast_to(t, (BLOCK, BLOCK)) for t in per_head], axis=1)


def _sb_tile(q, kb, vb, later, mask, tri2):
    z = lax.dot_general(q, _stack_heads(kb), (((1,), (1,)), ((), ())),
                        preferred_element_type=jnp.float32)
    soft = jnp.log2(1.0 + jnp.exp2(-jnp.abs(z)))
    log_surv = -(jnp.maximum(z, 0.0) + soft)
    log_beta = z + log_surv
    if mask is not None:
        log_surv = jnp.where(mask, log_surv, 0.0)
    hi, lo = _split_bf16(log_surv)
    w = jnp.dot(jnp.concatenate([hi, lo], axis=0), tri2, preferred_element_type=jnp.float32)
    within = w[:BLOCK] + w[BLOCK:]
    a = jnp.exp2(log_beta + within + later)
    if mask is not None:
        a = jnp.where(mask, a, 0.0)
    pv = jnp.dot(a.astype(jnp.bfloat16), _stack_heads(vb), preferred_element_type=jnp.float32)
    tot = [jnp.sum(log_surv[:, hd * BLOCK:(hd + 1) * BLOCK], axis=-1, keepdims=True)
           for hd in range(2)]
    return pv, tot


def _sb_attn_kernel(q_ref, k_ref, v_ref, gate_ref, tri2_ref, o_ref, acc_ref, later_ref):
    step = pl.program_id(0)
    tri2 = tri2_ref[...]
    shape2 = (BLOCK, 2 * BLOCK)
    strictly_before = _lane_iota(shape2) % BLOCK < _row_iota(shape2)

    def key_rows(j):
        return pl.ds(pl.multiple_of(j * BLOCK, BLOCK), BLOCK)

    qs, more = [], []
    for sub in range(SB_Q_BLOCKS):
        i = step * SB_Q_BLOCKS + sub
        rows = slice(sub * BLOCK, (sub + 1) * BLOCK)
        diag, prev = key_rows(i), key_rows(jnp.maximum(i - 1, 0))
        no_prev = jnp.where(i > 0, 0.0, NEG_BIG)
        worst = None
        for pair in range(N_PAIR_SB):
            lanes = slice(pair * LANES, (pair + 1) * LANES)
            slot = sub * N_PAIR_SB + pair
            q = q_ref[rows, lanes]
            pv0, tot0 = _sb_tile(q, k_ref[diag, lanes], v_ref[diag, lanes], 0.0,
                                 strictly_before, tri2)
            pv1, tot1 = _sb_tile(q, k_ref[prev, lanes], v_ref[prev, lanes],
                                 _spread(tot0) + no_prev, None, tri2)
            later = _spread([tot0[0] + tot1[0], tot0[1] + tot1[1]])
            acc_ref[slot] = pv0 + pv1
            later_ref[slot] = later
            qs.append(q)
            worst = later if worst is None else jnp.maximum(worst, later)
        more.append(jnp.max(worst) > LOG_SURVIVAL_FLOOR)

    for sub in range(SB_Q_BLOCKS):
        def cond(carry):
            j, live = carry
            return jnp.logical_and(j >= 0, live)

        def body(carry, sub=sub):
            j, _ = carry
            rows = key_rows(j)
            worst = None
            for pair in range(N_PAIR_SB):
                lanes = slice(pair * LANES, (pair + 1) * LANES)
                slot = sub * N_PAIR_SB + pair
                later = later_ref[slot]
                pv, tot = _sb_tile(qs[slot], k_ref[rows, lanes], v_ref[rows, lanes], later, None, tri2)
                acc_ref[slot] += pv
                later = later + _spread(tot)
                later_ref[slot] = later
                worst = later if worst is None else jnp.maximum(worst, later)
            return j - 1, jnp.max(worst) > LOG_SURVIVAL_FLOOR

        lax.while_loop(cond, body, (step * SB_Q_BLOCKS + sub - 2, more[sub]))

    for sub in range(SB_Q_BLOCKS):
        rows = slice(sub * BLOCK, (sub + 1) * BLOCK)
        for pair in range(N_PAIR_SB):
            lanes = slice(pair * LANES, (pair + 1) * LANES)
            gate = gate_ref[rows, lanes].astype(jnp.float32)
            o_ref[rows, lanes] = (acc_ref[sub * N_PAIR_SB + pair] * gate).astype(o_ref.dtype)


def _sb_attn(qa, ka, va, gate):
    s = qa.shape[0]
    tq = SB_Q_BLOCKS * BLOCK
    blk = lambda i: (i, 0)
    resident = pl.BlockSpec((s, W_SB), lambda i: (0, 0), pipeline_mode=pl.Buffered(1))
    idx = np.arange(2 * BLOCK)
    tri2 = jnp.asarray((idx[:, None] > idx[None, :]) & (idx[:, None] // BLOCK == idx[None, :] // BLOCK),
                       dtype=jnp.bfloat16)
    n_slots = SB_Q_BLOCKS * N_PAIR_SB
    return pl.pallas_call(
        _sb_attn_kernel,
        grid=(s // tq,),
        in_specs=[pl.BlockSpec((tq, W_SB), blk), resident, resident,
                  pl.BlockSpec((tq, W_SB), blk),
                  pl.BlockSpec(tri2.shape, lambda i: (0, 0))],
        out_specs=pl.BlockSpec((tq, W_SB), blk),
        out_shape=jax.ShapeDtypeStruct((s, W_SB), jnp.bfloat16),
        scratch_shapes=[pltpu.VMEM((n_slots, BLOCK, LANES), jnp.float32),
                        pltpu.VMEM((n_slots, BLOCK, 2 * BLOCK), jnp.float32)],
        compiler_params=pltpu.CompilerParams(
            dimension_semantics=("arbitrary",), vmem_limit_bytes=VMEM_LIMIT_BYTES),
        name="sb_attn",
    )(qa, ka, va, gate, tri2)


def _band_scores(q, k_prev, k_own, bias):
    kk = jnp.concatenate([k_prev, k_own], axis=0)
    return lax.dot_general(_stack_heads(q), kk, (((1,), (1,)), ((), ())),
                           preferred_element_type=jnp.float32) + bias


def _band_softmax_pv(s_ref, p_ref, v_prev, v_own):
    vv = jnp.concatenate([v_prev, v_own], axis=0)
    vv = jnp.concatenate([vv, jnp.ones_like(vv)], axis=1)
    ms = []
    for r0 in range(0, 2 * BLOCK, SOFTMAX_ROWS):
        rows = slice(r0, r0 + SOFTMAX_ROWS)
        s = s_ref[rows, :]
        m = jnp.max(s, axis=-1, keepdims=True)
        ms.append(m)
        p_ref[rows, :] = jnp.exp2(s - m).astype(jnp.bfloat16)
    ul = jnp.dot(p_ref[...], vv, preferred_element_type=jnp.float32)

    def unstack_columns(cols):
        half = len(cols) // 2
        head0 = _lane_iota((SOFTMAX_ROWS, LANES)) < HEAD_DIM
        return jnp.concatenate(
            [jnp.where(head0, jnp.broadcast_to(cols[c], (SOFTMAX_ROWS, LANES)),
                       jnp.broadcast_to(cols[half + c], (SOFTMAX_ROWS, LANES)))
             for c in range(half)], axis=0)

    return _unstack_heads(ul[:, :LANES]), unstack_columns(ms), _unstack_heads(ul[:, LANES:])


def _dil_attn_kernel(q1_ref, k1_ref, v1_ref, k1p_ref, v1p_ref,
                     q4_ref, k4_ref, v4_ref, k4p_ref, v4p_ref,
                     q16_ref, k16_ref, v16_ref, k16p_ref, v16p_ref,
                     gate_ref, bias_ref, o_ref, out_s, max_s, den_s, score_s, prob_s, stage_s):
    not_first_unit = pl.program_id(0) > 0
    blocks_per_unit = UNIT // BLOCK
    per_class4 = blocks_per_unit // 4

    def block_rows(n):
        return slice(n * BLOCK, (n + 1) * BLOCK), slice(max(n - 1, 0) * BLOCK, max(n, 1) * BLOCK)

    unit_bias = not_first_unit.astype(jnp.int32)

    def scores(t, slot):
        cur, prv = block_rows(t)
        k_prev = k1p_ref[...] if t == 0 else k1_ref[prv, :]
        score_s[slot, 0] = _band_scores(q1_ref[cur, :], k_prev, k1_ref[cur, :],
                                        bias_ref[unit_bias] if t == 0 else bias_ref[1])
        cls, n = divmod(t, per_class4)
        cur, prv = block_rows(n)
        k_prev = k4p_ref[cls] if n == 0 else k4_ref[cls, prv, :]
        score_s[slot, 1] = _band_scores(q4_ref[cls, cur, :], k_prev, k4_ref[cls, cur, :],
                                        bias_ref[unit_bias] if n == 0 else bias_ref[1])
        score_s[slot, 2] = _band_scores(q16_ref[t], k16p_ref[t], k16_ref[t], bias_ref[unit_bias])

    def put(pattern, rows, res):
        u, m, l = res
        out_s[pattern, rows, :] = u
        max_s[pattern, rows, :] = m
        den_s[pattern, rows, :] = l

    def softmax_pv(t, slot):
        cur, prv = block_rows(t)
        v_prev = v1p_ref[...] if t == 0 else v1_ref[prv, :]
        put(0, cur, _band_softmax_pv(score_s.at[slot, 0], prob_s.at[slot, 0], v_prev, v1_ref[cur, :]))
        cls, n = divmod(t, per_class4)
        cur, prv = block_rows(n)
        v_prev = v4p_ref[cls] if n == 0 else v4_ref[cls, prv, :]
        put(1, pl.ds(cls + 4 * BLOCK * n, BLOCK, stride=4), _band_softmax_pv(
            score_s.at[slot, 1], prob_s.at[slot, 1], v_prev, v4_ref[cls, cur, :]))
        res16 = _band_softmax_pv(score_s.at[slot, 2], prob_s.at[slot, 2], v16p_ref[t], v16_ref[t])
        for kind, val in enumerate(res16):
            stage_s[kind, t % 4, pl.ds(t // 4, BLOCK, stride=4), :] = val

    n_slots = score_s.shape[0]
    scores(0, 0)
    for t in range(blocks_per_unit):
        if t + 1 < blocks_per_unit:
            scores(t + 1, (t + 1) % n_slots)
        softmax_pv(t, t % n_slots)

    for c4 in range(4):
        for piece in range(UNIT // 4 // BLOCK):
            src = slice(piece * BLOCK, (piece + 1) * BLOCK)
            rows = pl.ds(c4 + 4 * BLOCK * piece, BLOCK, stride=4)
            out_s[2, rows, :] = stage_s[0, c4, src, :]
            max_s[2, rows, :] = stage_s[1, c4, src, :]
            den_s[2, rows, :] = stage_s[2, c4, src, :]

    chunk = 256
    for c0 in range(0, UNIT, chunk):
        rows = slice(c0, c0 + chunk)
        m0, m1, m2 = max_s[0, rows, :], max_s[1, rows, :], max_s[2, rows, :]
        mx = jnp.maximum(jnp.maximum(m0, m1), m2)
        w0, w1, w2 = jnp.exp2(m0 - mx), jnp.exp2(m1 - mx), jnp.exp2(m2 - mx)
        num = w0 * out_s[0, rows, :] + w1 * out_s[1, rows, :] + w2 * out_s[2, rows, :]
        den = w0 * den_s[0, rows, :] + w1 * den_s[1, rows, :] + w2 * den_s[2, rows, :]
        o_ref[rows, :] = (num / den * gate_ref[rows, :].astype(jnp.float32)).astype(o_ref.dtype)


def _band_bias_table():
    i = np.arange(2 * BLOCK)[:, None] % BLOCK
    j = np.arange(2 * BLOCK)[None, :]
    band = (j >= i) & (j <= i + BLOCK)
    table = np.stack([band & (j >= BLOCK), band])
    return jnp.asarray(np.where(table, 0.0, NEG_BIG), dtype=jnp.float32)


def _dil_attn(q1, k1, v1, q4, k4, v4, q16, k16, v16, gate):
    s = q1.shape[1]
    n_units = s // UNIT
    blocks_per_unit = UNIT // BLOCK
    n4 = UNIT // 4
    prev_unit = lambda u: jnp.maximum(u - 1, 0)
    nat = pl.BlockSpec((None, UNIT, LANES), lambda u, p: (p, u, 0))
    nat_prev = pl.BlockSpec((None, BLOCK, LANES),
                            lambda u, p: (p, jnp.maximum(u * blocks_per_unit - 1, 0), 0))
    c4 = pl.BlockSpec((None, None, 4, n4, LANES), lambda u, p: (p, u, 0, 0, 0))
    c4_prev = pl.BlockSpec((None, None, 4, BLOCK, LANES),
                           lambda u, p: (p, prev_unit(u), 0, n4 // BLOCK - 1, 0))
    c16 = pl.BlockSpec((None, None, 16, BLOCK, LANES), lambda u, p: (p, u, 0, 0, 0))
    c16_prev = pl.BlockSpec((None, None, 16, BLOCK, LANES), lambda u, p: (p, prev_unit(u), 0, 0, 0))
    bias = _band_bias_table()
    return pl.pallas_call(
        _dil_attn_kernel,
        grid=(n_units, N_PAIR_DIL),
        in_specs=[nat, nat, nat, nat_prev, nat_prev,
                  c4, c4, c4, c4_prev, c4_prev,
                  c16, c16, c16, c16_prev, c16_prev,
                  nat, pl.BlockSpec(bias.shape, lambda u, p: (0, 0, 0))],
        out_specs=nat,
        out_shape=jax.ShapeDtypeStruct((N_PAIR_DIL, s, LANES), jnp.bfloat16),
        scratch_shapes=[pltpu.VMEM((3, UNIT, LANES), jnp.float32),
                        pltpu.VMEM((3, UNIT, LANES), jnp.float32),
                        pltpu.VMEM((3, UNIT, LANES), jnp.float32),
                        pltpu.VMEM((PIPELINE_SLOTS, 3, 2 * BLOCK, 2 * BLOCK), jnp.float32),
                        pltpu.VMEM((PIPELINE_SLOTS, 3, 2 * BLOCK, 2 * BLOCK), jnp.bfloat16),
                        pltpu.VMEM((3, 4, UNIT // 4, LANES), jnp.float32)],
        compiler_params=pltpu.CompilerParams(
            dimension_semantics=("arbitrary", "arbitrary"), vmem_limit_bytes=VMEM_LIMIT_BYTES),
        name="dil_attn",
    )(q1, k1, v1, k1, v1, q4, k4, v4, k4, v4, q16, k16, v16, k16, v16, gate, bias)


def _out_proj_kernel(x_ref, ma_ref, md_ref, w_ref, o_ref):
    o_ref[...] = x_ref[...] + _out_proj_product(ma_ref, md_ref, w_ref)


def _out_proj(layer, x, mix_a, mix_d, w_bf16):
    s, d = x.shape
    tm = ROW_TILE
    row = lambda i: (i, 0)
    return pl.pallas_call(
        _out_proj_kernel,
        grid=(s // tm,),
        in_specs=[pl.BlockSpec((tm, d), row), pl.BlockSpec((tm, W_SB), row),
                  pl.BlockSpec((N_PAIR_DIL, tm, LANES), lambda i: (0, i, 0)),
                  pl.BlockSpec((None,) + w_bf16.shape[1:], lambda i: (layer, 0, 0))],
        out_specs=pl.BlockSpec((tm, d), row),
        out_shape=jax.ShapeDtypeStruct((s, d), x.dtype),
        compiler_params=pltpu.CompilerParams(
            dimension_semantics=("arbitrary",), vmem_limit_bytes=VMEM_LIMIT_BYTES),
        name="out_proj",
    )(x, mix_a, mix_d, w_bf16)


def _rope_tables(s):
    inv_freq = 1.0 / (ROPE_THETA ** (jnp.arange(ROPE_HALF, dtype=jnp.float32) * 2.0 / ROPE_DIM))
    dim = np.arange(LANES) % HEAD_DIM
    ang = jnp.arange(s).astype(jnp.float32)[:, None] * inv_freq[dim % ROPE_HALF][None, :]
    cos, sin = jnp.cos(ang), jnp.sin(ang)
    first, second = (dim < ROPE_HALF)[None, :], ((dim >= ROPE_HALF) & (dim < ROPE_DIM))[None, :]
    cos_t = jnp.where(first | second, cos, 1.0)
    sin_a = jnp.where(first, -sin, 0.0)
    sin_b = jnp.where(second, sin, 0.0)
    return cos_t, sin_a, sin_b


def kernel(x, norm_g, w_in, q_norm_g, k_norm_g, w_out):
    b, s, d = x.shape
    assert s % UNIT == 0 and d == W_SB + W_DIL
    depth = w_in.shape[0]
    cos_t, sin_a, sin_b = _rope_tables(s)
    w_in_bf = w_in.astype(jnp.bfloat16)
    w_out_bf = w_out.astype(jnp.bfloat16)
    two = lambda g: jnp.tile(g.reshape(depth, 1, HEAD_DIM), (1, 1, LANES // HEAD_DIM))
    norm_g3, q_gain, k_gain = norm_g.reshape(depth, 1, d), two(q_norm_g), two(k_norm_g)
    outs = []
    for bi in range(b):
        xb = x[bi]
        prev = None
        for layer in range(depth):
            res = _in_proj(layer, xb, norm_g3, w_in_bf, q_gain, k_gain, cos_t, sin_a, sin_b, prev)
            if prev is not None:
                xb, res = res[0], res[1:]
            (qa, ka, va, gate_a, gate_d, q1, k1, v1, q4, k4, v4, q16, k16, v16) = res
            mix_a = _sb_attn(qa, ka, va, gate_a)
            mix_d = _dil_attn(q1, k1, v1, q4, k4, v4, q16, k16, v16, gate_d)
            prev = (mix_a, mix_d, w_out_bf)
        outs.append(_out_proj(depth - 1, xb, mix_a, mix_d, w_out_bf))
    return outs[0][None] if b == 1 else jnp.stack(outs, axis=0)
```

```python
import math

import jax
import jax.numpy as jnp
import numpy as np
from jax import lax
from jax.experimental import pallas as pl
from jax.experimental.pallas import tpu as pltpu

HEAD_DIM = 64
N_HEADS_SB = 4
N_HEADS_DIL = 12
W_SB = N_HEADS_SB * HEAD_DIM
W_DIL = N_HEADS_DIL * HEAD_DIM
DIL_PATTERNS = ((128, 1), (512, 4), (2048, 16))
ROPE_THETA = 500000.0
ROPE_DIM = HEAD_DIM // 4
ROPE_HALF = ROPE_DIM // 2
BLOCK = 128
EPS = 1e-6
QK_SCALE = 1.0 / math.sqrt(HEAD_DIM)
LOG2_E = math.log2(math.e)

LANES = 128
UNIT = 16 * BLOCK
N_PAIR_DIL = W_DIL // LANES
N_PAIR_SB = W_SB // LANES
ROW_TILE = 512
SOFTMAX_ROWS = 64
SB_Q_BLOCKS = 8
PIPELINE_SLOTS = 2
BLOCKS_PER_STAGE = 2
VMEM_LIMIT_BYTES = 56 * 1024 * 1024

LOG_SURVIVAL_FLOOR = -90.0 * LOG2_E
NEG_BIG = -1e30


def _split_bf16(x):
    hi = x.astype(jnp.bfloat16)
    lo = (x - hi.astype(jnp.float32)).astype(jnp.bfloat16)
    return hi, lo


def _lane_iota(shape):
    return lax.broadcasted_iota(jnp.int32, shape, 1)


def _row_iota(shape):
    return lax.broadcasted_iota(jnp.int32, shape, 0)


def _silu(x):
    return x * (1.0 / (1.0 + jnp.exp(-x)))


def _stack_heads(t):
    lane = _lane_iota(t.shape)
    zero = jnp.zeros_like(t)
    return jnp.concatenate([jnp.where(lane < HEAD_DIM, t, zero),
                            jnp.where(lane >= HEAD_DIM, t, zero)], axis=0)


def _unstack_heads(t):
    n = t.shape[0] // 2
    top, bot = t[:n], t[n:]
    return jnp.where(_lane_iota(top.shape) < HEAD_DIM, top, bot)


def _out_proj_product(ma_ref, md_ref, wo_ref):
    mix_d = jnp.concatenate([md_ref[pair] for pair in range(N_PAIR_DIL)], axis=1)
    y = jnp.dot(ma_ref[...], wo_ref[0:W_SB, :], preferred_element_type=jnp.float32)
    return y + jnp.dot(mix_d, wo_ref[W_SB:, :], preferred_element_type=jnp.float32)


def _in_proj_kernel(x_ref, *refs):
    _in_proj_body(x_ref[...], *refs)


def _out_in_proj_kernel(x_ref, ma_ref, md_ref, wo_ref, *refs):
    (g_ref, w_ref, qg_ref, kg_ref, cos_ref, sa_ref, sb_ref, xn_ref, *out_and_scratch) = refs
    x = x_ref[...] + _out_proj_product(ma_ref, md_ref, wo_ref)
    xn_ref[...] = x
    _in_proj_body(x, g_ref, w_ref, qg_ref, kg_ref, cos_ref, sa_ref, sb_ref, *out_and_scratch)


def _in_proj_body(x, g_ref, w_ref, qg_ref, kg_ref, cos_ref, sa_ref, sb_ref,
                  qa_ref, ka_ref, va_ref, gate_a_ref, gate_d_ref,
                  q1_ref, k1_ref, v1_ref, q4_ref, k4_ref, v4_ref, q16_ref, k16_ref, v16_ref,
                  slab_ref, slab4_ref):
    tm = x.shape[0]
    ms = jnp.mean(x * x, axis=-1, keepdims=True)
    h = (x * lax.rsqrt(ms + EPS) * g_ref[...]).astype(jnp.bfloat16)

    def proj(c0, width):
        return jnp.dot(h, w_ref[:, c0:c0 + width], preferred_element_type=jnp.float32)

    base = 4 * W_SB

    r = (_row_iota((2 * LANES, LANES)) % LANES) // HEAD_DIM
    c = _lane_iota((2 * LANES, LANES)) // HEAD_DIM
    head_mean = jnp.where(r == c, 1.0 / HEAD_DIM, 0.0).astype(jnp.bfloat16)

    cos = cos_ref[...]
    sin_a = sa_ref[...]
    sin_b = sb_ref[...]

    def emit(pair, t, o1_ref, o4_ref, o16_ref):
        o1_ref[pair] = t.astype(jnp.bfloat16)
        slab_ref[pair] = t
        n4 = tm // 4
        for c4 in range(4):
            t4 = slab_ref[pair, pl.ds(c4, n4, stride=4), :]
            o4_ref[pair, c4] = t4.astype(jnp.bfloat16)
            slab4_ref[pair, c4 * n4:(c4 + 1) * n4, :] = t4
        n16 = tm // 16
        for c4 in range(4):
            for b in range(4):
                t16 = slab4_ref[pair, pl.ds(c4 * n4 + b, n16, stride=4), :]
                o16_ref[pair, c4 + 4 * b] = t16.astype(jnp.bfloat16)

    def qk_norm_rope(t, gain, scale):
        hi, lo = _split_bf16(t * t)
        msq = jnp.dot(jnp.concatenate([hi, lo], axis=1), head_mean,
                      preferred_element_type=jnp.float32)
        t = t * lax.rsqrt(msq + EPS) * gain
        up = pltpu.roll(t, LANES - ROPE_HALF, axis=1)
        down = pltpu.roll(t, ROPE_HALF, axis=1)
        t = t * cos + up * sin_a + down * sin_b
        if scale != 1.0:
            t = t * scale
        return t

    pq = proj(base, W_DIL)
    for pair in range(N_PAIR_DIL):
        t = qk_norm_rope(pq[:, pair * LANES:(pair + 1) * LANES], qg_ref[...], QK_SCALE * LOG2_E)
        emit(pair, t, q1_ref, q4_ref, q16_ref)
    pk = proj(base + W_DIL, W_DIL)
    for pair in range(N_PAIR_DIL):
        t = qk_norm_rope(pk[:, pair * LANES:(pair + 1) * LANES], kg_ref[...], 1.0)
        emit(pair, t, k1_ref, k4_ref, k16_ref)
    pv = proj(base + 2 * W_DIL, W_DIL)
    for pair in range(N_PAIR_DIL):
        emit(pair, pv[:, pair * LANES:(pair + 1) * LANES], v1_ref, v4_ref, v16_ref)

    pa = proj(0, 4 * W_SB)
    qa_ref[...] = (pa[:, 0:W_SB] * (QK_SCALE * LOG2_E)).astype(jnp.bfloat16)
    ka_ref[...] = pa[:, W_SB:2 * W_SB].astype(jnp.bfloat16)
    va_ref[...] = pa[:, 2 * W_SB:3 * W_SB].astype(jnp.bfloat16)
    gate_a_ref[...] = _silu(pa[:, 3 * W_SB:]).astype(jnp.bfloat16)
    gate_d = _silu(proj(base + 3 * W_DIL, W_DIL)).astype(jnp.bfloat16)
    for pair in range(N_PAIR_DIL):
        gate_d_ref[pair] = gate_d[:, pair * LANES:(pair + 1) * LANES]


def _in_proj(layer, x, g, w_bf16, qg, kg, cos_t, sin_a, sin_b, prev=None):
    s, d = x.shape
    tm = ROW_TILE
    per_unit = UNIT // tm
    n_units = s // UNIT
    bf = jnp.bfloat16
    row = lambda i: (i, 0)
    nat_spec = pl.BlockSpec((N_PAIR_DIL, tm, LANES), lambda i: (0, i, 0))
    c4_spec = pl.BlockSpec((N_PAIR_DIL, None, 4, tm // 4, LANES),
                           lambda i: (0, i // per_unit, 0, i % per_unit, 0))
    c16_spec = pl.BlockSpec((N_PAIR_DIL, None, 16, tm // 16, LANES),
                            lambda i: (0, i // per_unit, 0, i % per_unit, 0))
    nat_shape = jax.ShapeDtypeStruct((N_PAIR_DIL, s, LANES), bf)
    c4_shape = jax.ShapeDtypeStruct((N_PAIR_DIL, n_units, 4, UNIT // 4, LANES), bf)
    c16_shape = jax.ShapeDtypeStruct((N_PAIR_DIL, n_units, 16, UNIT // 16, LANES), bf)
    sb_shape = jax.ShapeDtypeStruct((s, W_SB), bf)
    in_specs = [
        pl.BlockSpec((None, 1, d), lambda i: (layer, 0, 0)),
        pl.BlockSpec((None,) + w_bf16.shape[1:], lambda i: (layer, 0, 0)),
        pl.BlockSpec((None, 1, LANES), lambda i: (layer, 0, 0)),
        pl.BlockSpec((None, 1, LANES), lambda i: (layer, 0, 0)),
        pl.BlockSpec((tm, LANES), row),
        pl.BlockSpec((tm, LANES), row),
        pl.BlockSpec((tm, LANES), row),
    ]
    operands = [g, w_bf16, qg, kg, cos_t, sin_a, sin_b]
    out_specs = [
        pl.BlockSpec((tm, W_SB), row), pl.BlockSpec((tm, W_SB), row),
        pl.BlockSpec((tm, W_SB), row), pl.BlockSpec((tm, W_SB), row), nat_spec,
        nat_spec, nat_spec, nat_spec, c4_spec, c4_spec, c4_spec,
        c16_spec, c16_spec, c16_spec,
    ]
    out_shape = [sb_shape, sb_shape, sb_shape, sb_shape, nat_shape,
                 nat_shape, nat_shape, nat_shape, c4_shape, c4_shape, c4_shape,
                 c16_shape, c16_shape, c16_shape]
    x_spec = pl.BlockSpec((tm, d), row)
    if prev is None:
        body, name = _in_proj_kernel, "in_proj"
        in_specs = [x_spec] + in_specs
        operands = [x] + operands
    else:
        mix_a, mix_d, w_out = prev
        body, name = _out_in_proj_kernel, "out_in_proj"
        in_specs = [x_spec, pl.BlockSpec((tm, W_SB), row), nat_spec,
                    pl.BlockSpec((None,) + w_out.shape[1:], lambda i: (layer - 1, 0, 0))] + in_specs
        operands = [x, mix_a, mix_d, w_out] + operands
        out_specs = [x_spec] + out_specs
        out_shape = [jax.ShapeDtypeStruct((s, d), x.dtype)] + out_shape
    return pl.pallas_call(
        body,
        grid=(s // tm,),
        in_specs=in_specs,
        out_specs=out_specs,
        out_shape=out_shape,
        scratch_shapes=[pltpu.VMEM((N_PAIR_DIL, tm, LANES), jnp.float32),
                        pltpu.VMEM((N_PAIR_DIL, tm, LANES), jnp.float32)],
        compiler_params=pltpu.CompilerParams(
            dimension_semantics=("arbitrary",), vmem_limit_bytes=VMEM_LIMIT_BYTES),
        name=name,
    )(*operands)


def _spread(per_head):
    return jnp.concatenate([jnp.broadcast_to(t, (BLOCK, BLOCK)) for t in per_head], axis=1)


def _sb_tile(q, kb, vb, later, mask, tri2):
    z = lax.dot_general(q, _stack_heads(kb), (((1,), (1,)), ((), ())),
                        preferred_element_type=jnp.float32)
    soft = jnp.log2(1.0 + jnp.exp2(-jnp.abs(z)))
    log_surv = -(jnp.maximum(z, 0.0) + soft)
    log_beta = z + log_surv
    if mask is not None:
        log_surv = jnp.where(mask, log_surv, 0.0)
    hi, lo = _split_bf16(log_surv)
    w = jnp.dot(jnp.concatenate([hi, lo], axis=0), tri2, preferred_element_type=jnp.float32)
    within = w[:BLOCK] + w[BLOCK:]
    a = jnp.exp2(log_beta + within + later)
    if mask is not None:
        a = jnp.where(mask, a, 0.0)
    pv = jnp.dot(a.astype(jnp.bfloat16), _stack_heads(vb), preferred_element_type=jnp.float32)
    tot = [jnp.sum(log_surv[:, hd * BLOCK:(hd + 1) * BLOCK], axis=-1, keepdims=True)
           for hd in range(2)]
    return pv, tot


def _sb_attn_kernel(q_ref, k_ref, v_ref, gate_ref, tri2_ref, o_ref, acc_ref, later_ref):
    step = pl.program_id(0)
    tri2 = tri2_ref[...]
    shape2 = (BLOCK, 2 * BLOCK)
    strictly_before = _lane_iota(shape2) % BLOCK < _row_iota(shape2)

    def key_rows(j):
        return pl.ds(pl.multiple_of(j * BLOCK, BLOCK), BLOCK)

    qs, more = [], []
    for sub in range(SB_Q_BLOCKS):
        i = step * SB_Q_BLOCKS + sub
        rows = slice(sub * BLOCK, (sub + 1) * BLOCK)
        diag, prev = key_rows(i), key_rows(jnp.maximum(i - 1, 0))
        no_prev = jnp.where(i > 0, 0.0, NEG_BIG)
        worst = None
        for pair in range(N_PAIR_SB):
            lanes = slice(pair * LANES, (pair + 1) * LANES)
            slot = sub * N_PAIR_SB + pair
            q = q_ref[rows, lanes]
            pv0, tot0 = _sb_tile(q, k_ref[diag, lanes], v_ref[diag, lanes], 0.0,
                                 strictly_before, tri2)
            pv1, tot1 = _sb_tile(q, k_ref[prev, lanes], v_ref[prev, lanes],
                                 _spread(tot0) + no_prev, None, tri2)
            later = _spread([tot0[0] + tot1[0], tot0[1] + tot1[1]])
            acc_ref[slot] = pv0 + pv1
            later_ref[slot] = later
            qs.append(q)
            worst = later if worst is None else jnp.maximum(worst, later)
        more.append(jnp.max(worst) > LOG_SURVIVAL_FLOOR)

    for sub in range(SB_Q_BLOCKS):
        def cond(carry):
            j, live = carry
            return jnp.logical_and(j >= 0, live)

        def body(carry, sub=sub):
            j, _ = carry
            rows = key_rows(j)
            worst = None
            for pair in range(N_PAIR_SB):
                lanes = slice(pair * LANES, (pair + 1) * LANES)
                slot = sub * N_PAIR_SB + pair
                later = later_ref[slot]
                pv, tot = _sb_tile(qs[slot], k_ref[rows, lanes], v_ref[rows, lanes], later, None, tri2)
                acc_ref[slot] += pv
                later = later + _spread(tot)
                later_ref[slot] = later
                worst = later if worst is None else jnp.maximum(worst, later)
            return j - 1, jnp.max(worst) > LOG_SURVIVAL_FLOOR

        lax.while_loop(cond, body, (step * SB_Q_BLOCKS + sub - 2, more[sub]))

    for sub in range(SB_Q_BLOCKS):
        rows = slice(sub * BLOCK, (sub + 1) * BLOCK)
        for pair in range(N_PAIR_SB):
            lanes = slice(pair * LANES, (pair + 1) * LANES)
            gate = gate_ref[rows, lanes].astype(jnp.float32)
            o_ref[rows, lanes] = (acc_ref[sub * N_PAIR_SB + pair] * gate).astype(o_ref.dtype)


def _sb_attn(qa, ka, va, gate):
    s = qa.shape[0]
    tq = SB_Q_BLOCKS * BLOCK
    blk = lambda i: (i, 0)
    resident = pl.BlockSpec((s, W_SB), lambda i: (0, 0), pipeline_mode=pl.Buffered(1))
    idx = np.arange(2 * BLOCK)
    tri2 = jnp.asarray((idx[:, None] > idx[None, :]) & (idx[:, None] // BLOCK == idx[None, :] // BLOCK),
                       dtype=jnp.bfloat16)
    n_slots = SB_Q_BLOCKS * N_PAIR_SB
    return pl.pallas_call(
        _sb_attn_kernel,
        grid=(s // tq,),
        in_specs=[pl.BlockSpec((tq, W_SB), blk), resident, resident,
                  pl.BlockSpec((tq, W_SB), blk),
                  pl.BlockSpec(tri2.shape, lambda i: (0, 0))],
        out_specs=pl.BlockSpec((tq, W_SB), blk),
        out_shape=jax.ShapeDtypeStruct((s, W_SB), jnp.bfloat16),
        scratch_shapes=[pltpu.VMEM((n_slots, BLOCK, LANES), jnp.float32),
                        pltpu.VMEM((n_slots, BLOCK, 2 * BLOCK), jnp.float32)],
        compiler_params=pltpu.CompilerParams(
            dimension_semantics=("arbitrary",), vmem_limit_bytes=VMEM_LIMIT_BYTES),
        name="sb_attn",
    )(qa, ka, va, gate, tri2)


def _band_scores(q, k_prev, k_own, bias):
    kk = jnp.concatenate([k_prev, k_own], axis=0)
    return lax.dot_general(_stack_heads(q), kk, (((1,), (1,)), ((), ())),
                           preferred_element_type=jnp.float32) + bias


def _band_softmax_pv(s_ref, p_ref, v_prev, v_own):
    vv = jnp.concatenate([v_prev, v_own], axis=0)
    vv = jnp.concatenate([vv, jnp.ones_like(vv)], axis=1)
    ms = []
    for r0 in range(0, 2 * BLOCK, SOFTMAX_ROWS):
        rows = slice(r0, r0 + SOFTMAX_ROWS)
        s = s_ref[rows, :]
        m = jnp.max(s, axis=-1, keepdims=True)
        ms.append(m)
        p_ref[rows, :] = jnp.exp2(s - m).astype(jnp.bfloat16)
    ul = jnp.dot(p_ref[...], vv, preferred_element_type=jnp.float32)

    def unstack_columns(cols):
        half = len(cols) // 2
        head0 = _lane_iota((SOFTMAX_ROWS, LANES)) < HEAD_DIM
        return jnp.concatenate(
            [jnp.where(head0, jnp.broadcast_to(cols[c], (SOFTMAX_ROWS, LANES)),
                       jnp.broadcast_to(cols[half + c], (SOFTMAX_ROWS, LANES)))
             for c in range(half)], axis=0)

    return _unstack_heads(ul[:, :LANES]), unstack_columns(ms), _unstack_heads(ul[:, LANES:])


def _dil_attn_kernel(q1_ref, k1_ref, v1_ref, k1p_ref, v1p_ref,
                     q4_ref, k4_ref, v4_ref, k4p_ref, v4p_ref,
                     q16_ref, k16_ref, v16_ref, k16p_ref, v16p_ref,
                     gate_ref, bias_ref, o_ref, out_s, max_s, den_s, score_s, prob_s, stage_s):
    not_first_unit = pl.program_id(0) > 0
    blocks_per_unit = UNIT // BLOCK
    per_class4 = blocks_per_unit // 4

    def block_rows(n):
        return slice(n * BLOCK, (n + 1) * BLOCK), slice(max(n - 1, 0) * BLOCK, max(n, 1) * BLOCK)

    unit_bias = not_first_unit.astype(jnp.int32)

    def block_scores(dilation, t):
        if dilation == 1:
            cur, prv = block_rows(t)
            k_prev = k1p_ref[...] if t == 0 else k1_ref[prv, :]
            return _band_scores(q1_ref[cur, :], k_prev, k1_ref[cur, :],
                                bias_ref[unit_bias] if t == 0 else bias_ref[1])
        if dilation == 4:
            cls, n = divmod(t, per_class4)
            cur, prv = block_rows(n)
            k_prev = k4p_ref[cls] if n == 0 else k4_ref[cls, prv, :]
            return _band_scores(q4_ref[cls, cur, :], k_prev, k4_ref[cls, cur, :],
                                bias_ref[unit_bias] if n == 0 else bias_ref[1])
        return _band_scores(q16_ref[t], k16p_ref[t], k16_ref[t], bias_ref[unit_bias])

    def block_softmax_pv(dilation, t, s_ref, p_ref):
        if dilation == 1:
            cur, prv = block_rows(t)
            v_prev = v1p_ref[...] if t == 0 else v1_ref[prv, :]
            res = _band_softmax_pv(s_ref, p_ref, v_prev, v1_ref[cur, :])
            out_s[0, cur, :], max_s[0, cur, :], den_s[0, cur, :] = res
            combine(cur)
        elif dilation == 4:
            cls, n = divmod(t, per_class4)
            cur, prv = block_rows(n)
            v_prev = v4p_ref[cls] if n == 0 else v4_ref[cls, prv, :]
            res = _band_softmax_pv(s_ref, p_ref, v_prev, v4_ref[cls, cur, :])
            rows = pl.ds(cls + 4 * BLOCK * n, BLOCK, stride=4)
            out_s[1, rows, :], max_s[1, rows, :], den_s[1, rows, :] = res
        else:
            res = _band_softmax_pv(s_ref, p_ref, v16p_ref[t], v16_ref[t])
            for kind, val in enumerate(res):
                stage_s[kind, t % 4, pl.ds(t // 4, BLOCK, stride=4), :] = val
            if t == blocks_per_unit - 1:
                rescatter16()

    def rescatter16():
        for c4 in range(4):
            for piece in range(UNIT // 4 // BLOCK):
                src = slice(piece * BLOCK, (piece + 1) * BLOCK)
                rows = pl.ds(c4 + 4 * BLOCK * piece, BLOCK, stride=4)
                out_s[2, rows, :] = stage_s[0, c4, src, :]
                max_s[2, rows, :] = stage_s[1, c4, src, :]
                den_s[2, rows, :] = stage_s[2, c4, src, :]

    def combine(rows):
        m0, m1, m2 = max_s[0, rows, :], max_s[1, rows, :], max_s[2, rows, :]
        mx = jnp.maximum(jnp.maximum(m0, m1), m2)
        w0, w1, w2 = jnp.exp2(m0 - mx), jnp.exp2(m1 - mx), jnp.exp2(m2 - mx)
        num = w0 * out_s[0, rows, :] + w1 * out_s[1, rows, :] + w2 * out_s[2, rows, :]
        den = w0 * den_s[0, rows, :] + w1 * den_s[1, rows, :] + w2 * den_s[2, rows, :]
        o_ref[rows, :] = (num / den * gate_ref[rows, :].astype(jnp.float32)).astype(o_ref.dtype)

    order = ([(16, t) for t in range(blocks_per_unit)] + [(4, t) for t in range(blocks_per_unit)]
             + [(1, t) for t in range(blocks_per_unit)])
    groups = [order[g:g + BLOCKS_PER_STAGE] for g in range(0, len(order), BLOCKS_PER_STAGE)]
    n_slots = score_s.shape[0]

    def group_scores(g):
        for j, (dilation, t) in enumerate(groups[g]):
            score_s[g % n_slots, j] = block_scores(dilation, t)

    ahead = n_slots - 1
    for g in range(min(ahead, len(groups))):
        group_scores(g)
    for g in range(len(groups)):
        if g + ahead < len(groups):
            group_scores(g + ahead)
        for j, (dilation, t) in enumerate(groups[g]):
            block_softmax_pv(dilation, t, score_s.at[g % n_slots, j], prob_s.at[g % n_slots, j])


def _band_bias_table():
    i = np.arange(2 * BLOCK)[:, None] % BLOCK
    j = np.arange(2 * BLOCK)[None, :]
    band = (j >= i) & (j <= i + BLOCK)
    table = np.stack([band & (j >= BLOCK), band])
    return jnp.asarray(np.where(table, 0.0, NEG_BIG), dtype=jnp.float32)


def _dil_attn(q1, k1, v1, q4, k4, v4, q16, k16, v16, gate):
    s = q1.shape[1]
    n_units = s // UNIT
    blocks_per_unit = UNIT // BLOCK
    n4 = UNIT // 4
    prev_unit = lambda u: jnp.maximum(u - 1, 0)
    nat = pl.BlockSpec((None, UNIT, LANES), lambda u, p: (p, u, 0))
    nat_prev = pl.BlockSpec((None, BLOCK, LANES),
                            lambda u, p: (p, jnp.maximum(u * blocks_per_unit - 1, 0), 0))
    c4 = pl.BlockSpec((None, None, 4, n4, LANES), lambda u, p: (p, u, 0, 0, 0))
    c4_prev = pl.BlockSpec((None, None, 4, BLOCK, LANES),
                           lambda u, p: (p, prev_unit(u), 0, n4 // BLOCK - 1, 0))
    c16 = pl.BlockSpec((None, None, 16, BLOCK, LANES), lambda u, p: (p, u, 0, 0, 0))
    c16_prev = pl.BlockSpec((None, None, 16, BLOCK, LANES), lambda u, p: (p, prev_unit(u), 0, 0, 0))
    bias = _band_bias_table()
    return pl.pallas_call(
        _dil_attn_kernel,
        grid=(n_units, N_PAIR_DIL),
        in_specs=[nat, nat, nat, nat_prev, nat_prev,
                  c4, c4, c4, c4_prev, c4_prev,
                  c16, c16, c16, c16_prev, c16_prev,
                  nat, pl.BlockSpec(bias.shape, lambda u, p: (0, 0, 0))],
        out_specs=nat,
        out_shape=jax.ShapeDtypeStruct((N_PAIR_DIL, s, LANES), jnp.bfloat16),
        scratch_shapes=[pltpu.VMEM((3, UNIT, LANES), jnp.float32),
                        pltpu.VMEM((3, UNIT, LANES), jnp.float32),
                        pltpu.VMEM((3, UNIT, LANES), jnp.float32),
                        pltpu.VMEM((PIPELINE_SLOTS, BLOCKS_PER_STAGE, 2 * BLOCK, 2 * BLOCK), jnp.float32),
                        pltpu.VMEM((PIPELINE_SLOTS, BLOCKS_PER_STAGE, 2 * BLOCK, 2 * BLOCK), jnp.bfloat16),
                        pltpu.VMEM((3, 4, UNIT // 4, LANES), jnp.float32)],
        compiler_params=pltpu.CompilerParams(
            dimension_semantics=("arbitrary", "arbitrary"), vmem_limit_bytes=VMEM_LIMIT_BYTES),
        name="dil_attn",
    )(q1, k1, v1, k1, v1, q4, k4, v4, k4, v4, q16, k16, v16, k16, v16, gate, bias)


def _out_proj_kernel(x_ref, ma_ref, md_ref, w_ref, o_ref):
    o_ref[...] = x_ref[...] + _out_proj_product(ma_ref, md_ref, w_ref)


def _out_proj(layer, x, mix_a, mix_d, w_bf16):
    s, d = x.shape
    tm = ROW_TILE
    row = lambda i: (i, 0)
    return pl.pallas_call(
        _out_proj_kernel,
        grid=(s // tm,),
        in_specs=[pl.BlockSpec((tm, d), row), pl.BlockSpec((tm, W_SB), row),
                  pl.BlockSpec((N_PAIR_DIL, tm, LANES), lambda i: (0, i, 0)),
                  pl.BlockSpec((None,) + w_bf16.shape[1:], lambda i: (layer, 0, 0))],
        out_specs=pl.BlockSpec((tm, d), row),
        out_shape=jax.ShapeDtypeStruct((s, d), x.dtype),
        compiler_params=pltpu.CompilerParams(
            dimension_semantics=("arbitrary",), vmem_limit_bytes=VMEM_LIMIT_BYTES),
        name="out_proj",
    )(x, mix_a, mix_d, w_bf16)


def _rope_tables(s):
    inv_freq = 1.0 / (ROPE_THETA ** (jnp.arange(ROPE_HALF, dtype=jnp.float32) * 2.0 / ROPE_DIM))
    dim = np.arange(LANES) % HEAD_DIM
    ang = jnp.arange(s).astype(jnp.float32)[:, None] * inv_freq[dim % ROPE_HALF][None, :]
    cos, sin = jnp.cos(ang), jnp.sin(ang)
    first, second = (dim < ROPE_HALF)[None, :], ((dim >= ROPE_HALF) & (dim < ROPE_DIM))[None, :]
    cos_t = jnp.where(first | second, cos, 1.0)
    sin_a = jnp.where(first, -sin, 0.0)
    sin_b = jnp.where(second, sin, 0.0)
    return cos_t, sin_a, sin_b


def kernel(x, norm_g, w_in, q_norm_g, k_norm_g, w_out):
    b, s, d = x.shape
    assert s % UNIT == 0 and d == W_SB + W_DIL
    depth = w_in.shape[0]
    cos_t, sin_a, sin_b = _rope_tables(s)
    w_in_bf = w_in.astype(jnp.bfloat16)
    w_out_bf = w_out.astype(jnp.bfloat16)
    two = lambda g: jnp.tile(g.reshape(depth, 1, HEAD_DIM), (1, 1, LANES // HEAD_DIM))
    norm_g3, q_gain, k_gain = norm_g.reshape(depth, 1, d), two(q_norm_g), two(k_norm_g)
    outs = []
    for bi in range(b):
        xb = x[bi]
        prev = None
        for layer in range(depth):
            res = _in_proj(layer, xb, norm_g3, w_in_bf, q_gain, k_gain, cos_t, sin_a, sin_b, prev)
            if prev is not None:
                xb, res = res[0], res[1:]
            (qa, ka, va, gate_a, gate_d, q1, k1, v1, q4, k4, v4, q16, k16, v16) = res
            mix_a = _sb_attn(qa, ka, va, gate_a)
            mix_d = _dil_attn(q1, k1, v1, q4, k4, v4, q16, k16, v16, gate_d)
            prev = (mix_a, mix_d, w_out_bf)
        outs.append(_out_proj(depth - 1, xb, mix_a, mix_d, w_out_bf))
    return outs[0][None] if b == 1 else jnp.stack(outs, axis=0)
```
